```python
import jax
import jax.numpy as jnp
from jax import lax
import numpy as np

D_MODEL = 2048
BATCH = 1
SEQ = 8192
DEPTH = 1
DEC_BATCH = 32
DEC_SEQ = 1
PAST_LEN = 8192
PAGE_SIZE = 128

HEAD_DIM = 128
SB_HEADS = 8
DSA_HEADS = 8
SB_WIDTH = SB_HEADS * HEAD_DIM
DSA_WIDTH = DSA_HEADS * HEAD_DIM
IDX_HEADS = 16
IDX_DIM = 128
TOPK_MAX = 256
D_FF = 5632
CONV_W = 3
Q_BLOCK = 128
ROPE_THETA = 10000.0
NORM_EPS = 1e-6
IN_SIZES = (SB_WIDTH, SB_WIDTH, SB_WIDTH, DSA_WIDTH, DSA_WIDTH, DSA_WIDTH,
            IDX_HEADS * IDX_DIM, IDX_DIM, IDX_HEADS, D_MODEL, D_MODEL)
IN_WIDTH = sum(IN_SIZES)

kernel_name = 'stick_breaking_dsa_gated_hybrid_step'


def _rmsnorm(x, g):
    xf = x.astype(jnp.float32)
    xf = xf * lax.rsqrt(jnp.mean(xf * xf, axis=-1, keepdims=True) + NORM_EPS)
    return xf.astype(x.dtype) * g


def _rope_tables(pos, dim):
    inv = ROPE_THETA ** (-jnp.arange(0, dim, 2, dtype=jnp.float32) / dim)
    ang = pos.astype(jnp.float32)[:, None] * inv[None, :]
    return jnp.cos(ang), jnp.sin(ang)


def _rope(x, cos, sin):
    half = x.shape[-1] // 2
    x1, x2 = x[..., :half], x[..., half:]
    cos = cos.astype(x.dtype)
    sin = sin.astype(x.dtype)
    return jnp.concatenate([x1 * cos - x2 * sin, x2 * cos + x1 * sin], axis=-1)


def _project(h, w_in, pos):
    b, t, _ = h.shape
    pts = np.cumsum(IN_SIZES)[:-1].tolist()
    q_sb, k_sb, v_sb, q_d, k_d, v_d, q_i, k_i, w_i, g_sb, g_d = jnp.split(h @ w_in, pts, axis=-1)
    cos, sin = _rope_tables(pos, HEAD_DIM)
    ci, si = _rope_tables(pos, IDX_DIM)
    hd = lambda z, n, d: z.reshape(b, t, n, d)
    return dict(
        q_sb=hd(q_sb, SB_HEADS, HEAD_DIM), k_sb=hd(k_sb, SB_HEADS, HEAD_DIM), v_sb=hd(v_sb, SB_HEADS, HEAD_DIM),
        q_d=_rope(hd(q_d, DSA_HEADS, HEAD_DIM), cos[:, None], sin[:, None]),
        k_d=_rope(hd(k_d, DSA_HEADS, HEAD_DIM), cos[:, None], sin[:, None]),
        v_d=hd(v_d, DSA_HEADS, HEAD_DIM),
        q_i=_rope(hd(q_i, IDX_HEADS, IDX_DIM), ci[:, None], si[:, None]),
        k_i=_rope(k_i, ci, si),
        w_i=w_i * IDX_HEADS ** -0.5,
        g_sb=g_sb, g_d=g_d)


def _blocks(z, b, nb):
    return jnp.swapaxes(z.reshape((b, nb, Q_BLOCK) + z.shape[2:]), 0, 1)


def _sb_block(q, k, v, mask, carry_log):
    z = jnp.einsum('bqhd,bkhd->bhqk', q, k).astype(jnp.float32) * HEAD_DIM ** -0.5
    lg_neg = jnp.where(mask, jax.nn.log_sigmoid(-z), 0.0)
    suffix = lax.cumsum(lg_neg, axis=3, reverse=True) - lg_neg
    w = jnp.where(mask, jnp.exp(jax.nn.log_sigmoid(z) + suffix + carry_log[..., None]), 0.0)
    out = jnp.einsum('bhqk,bkhd->bqhd', w, v.astype(jnp.float32))
    return out, carry_log + jnp.sum(lg_neg, axis=3)


def _sb_prompt(q, k, v):
    b, s, h, d = q.shape
    nb = s // Q_BLOCK
    kpos = jnp.arange(s)

    def one(args):
        qb, bi = args
        qpos = bi * Q_BLOCK + jnp.arange(Q_BLOCK)
        mask = kpos[None, :] < qpos[:, None]
        out, _ = _sb_block(qb, k, v, mask, jnp.zeros((b, h, Q_BLOCK), jnp.float32))
        return out

    o = lax.map(one, (_blocks(q, b, nb), jnp.arange(nb)))
    return jnp.swapaxes(o, 0, 1).reshape(b, s, h, d)


def _sb_sample(q, k_new, v_new, cache_k, cache_v, page_table):
    b, t, h, d = q.shape
    mask_new = jnp.arange(t)[None, :] < jnp.arange(t)[:, None]
    out, carry = _sb_block(q, k_new, v_new, mask_new, jnp.zeros((b, h, t), jnp.float32))
    mask_page = jnp.ones((t, PAGE_SIZE), dtype=bool)

    def step(c, phys):
        acc, lc = c
        o, lc = _sb_block(q, cache_k[phys], cache_v[phys], mask_page, lc)
        return (acc + o, lc), None

    (out, _), _ = lax.scan(step, (out, carry), page_table.T[::-1])
    return out


def _index_scores(q_i, k_i, w_i):
    s = jnp.einsum('bthe,bse->bths', q_i, k_i).astype(jnp.float32) * IDX_DIM ** -0.5
    return jnp.einsum('bths,bth->bts', jax.nn.relu(s), w_i.astype(jnp.float32))


def _select(score, mask, qpos, topk):
    _, idx = lax.top_k(jnp.where(mask, score, -jnp.inf), topk)
    return idx, idx <= qpos[None, :, None]


def _sparse_attn(q, kg, vg, valid):
    s = jnp.einsum('bthd,btkhd->bhtk', q, kg).astype(jnp.float32) * HEAD_DIM ** -0.5
    p = jax.nn.softmax(jnp.where(valid[:, None], s, -jnp.inf), axis=-1)
    return jnp.einsum('bhtk,btkhd->bthd', p, vg.astype(jnp.float32))


def _dsa_prompt(q, k, v, q_i, k_i, w_i):
    b, s, h, d = q.shape
    nb = s // Q_BLOCK
    topk = min(TOPK_MAX, s // 4)
    kpos = jnp.arange(s)
    gather = jax.vmap(lambda rows, ids: rows[ids])

    def one(args):
        qb, qib, wib, bi = args
        qpos = bi * Q_BLOCK + jnp.arange(Q_BLOCK)
        mask = kpos[None, :] <= qpos[:, None]
        idx, valid = _select(_index_scores(qib, k_i, wib), mask, qpos, topk)
        return _sparse_attn(qb, gather(k, idx), gather(v, idx), valid)

    o = lax.map(one, (_blocks(q, b, nb), _blocks(q_i, b, nb), _blocks(w_i, b, nb), jnp.arange(nb)))
    return jnp.swapaxes(o, 0, 1).reshape(b, s, h, d)


def _dsa_sample(q, k_new, v_new, q_i, k_i_new, w_i, cache_k, cache_v, cache_ki, page_table):
    b, t = q.shape[:2]
    past = page_table.shape[1] * PAGE_SIZE
    n_keys = past + t
    topk = min(TOPK_MAX, n_keys // 4)
    ki_all = jnp.concatenate([cache_ki[page_table].reshape(b, past, IDX_DIM), k_i_new], axis=1)
    qpos = past + jnp.arange(t)
    mask = jnp.arange(n_keys)[None, :] <= qpos[:, None]
    idx, valid = _select(_index_scores(q_i, ki_all, w_i), mask, qpos, topk)
    in_past = (idx < past)[..., None, None]
    ip = jnp.minimum(idx, past - 1)
    phys = jnp.take_along_axis(page_table, (ip // PAGE_SIZE).reshape(b, -1), axis=1).reshape(idx.shape)
    row = ip % PAGE_SIZE
    inew = jnp.clip(idx - past, 0, t - 1)
    gather = jax.vmap(lambda rows, ids: rows[ids])
    kg = jnp.where(in_past, cache_k[phys, row], gather(k_new, inew))
    vg = jnp.where(in_past, cache_v[phys, row], gather(v_new, inew))
    return _sparse_attn(q, kg, vg, valid)


def _merge(o_sb, o_d, g_sb, g_d, w_proj_sb, w_proj_dsa, w_out):
    b, t = o_sb.shape[:2]
    dt = g_sb.dtype
    y_sb = o_sb.reshape(b, t, SB_WIDTH).astype(dt) @ w_proj_sb
    y_d = o_d.reshape(b, t, DSA_WIDTH).astype(dt) @ w_proj_dsa
    return (jax.nn.sigmoid(g_sb) * y_sb + jax.nn.sigmoid(g_d) * y_d) @ w_out


def _conv_ffn(h, conv_prev, w_up, conv_w, conv_b, w_down):
    t = h.shape[1]
    buf = jnp.concatenate([conv_prev, h @ w_up], axis=1)
    c = conv_b
    for j in range(CONV_W):
        c = c + buf[:, j:j + t] * conv_w[j]
    g, v = jnp.split(c, 2, axis=-1)
    return (jax.nn.silu(g) * v) @ w_down, buf[:, buf.shape[1] - (CONV_W - 1):]


def setup_inputs(seed: int = 0) -> dict:
    key = jax.random.key(seed)
    ks = jax.random.split(key, 24)
    n_pages = PAST_LEN // PAGE_SIZE
    n_used = DEC_BATCH * n_pages
    n_pool = n_used + max(1, n_used // 4)
    nrm = lambda k, shape, scale=1.0: scale * jax.random.normal(k, shape, jnp.float32)
    page_table = jax.random.permutation(ks[0], n_pool)[:n_used].reshape(DEC_BATCH, n_pages).astype(jnp.int32)
    return {
        'x_prompt': nrm(ks[1], (BATCH, SEQ, D_MODEL)),
        'x_sample': nrm(ks[2], (DEC_BATCH, DEC_SEQ, D_MODEL)),
        'cache_k_sb': nrm(ks[3], (DEPTH, n_pool, PAGE_SIZE, SB_HEADS, HEAD_DIM)),
        'cache_v_sb': nrm(ks[4], (DEPTH, n_pool, PAGE_SIZE, SB_HEADS, HEAD_DIM)),
        'cache_k_dsa': nrm(ks[5], (DEPTH, n_pool, PAGE_SIZE, DSA_HEADS, HEAD_DIM)),
        'cache_v_dsa': nrm(ks[6], (DEPTH, n_pool, PAGE_SIZE, DSA_HEADS, HEAD_DIM)),
        'cache_k_idx': nrm(ks[7], (DEPTH, n_pool, PAGE_SIZE, IDX_DIM)),
        'state_conv': nrm(ks[8], (DEPTH, DEC_BATCH, CONV_W - 1, 2 * D_FF)),
        'page_table': page_table,
        'norm_mix': 1.0 + nrm(ks[9], (DEPTH, D_MODEL), 0.01),
        'w_in': nrm(ks[10], (DEPTH, D_MODEL, IN_WIDTH), D_MODEL ** -0.5),
        'w_proj_sb': nrm(ks[11], (DEPTH, SB_WIDTH, D_MODEL), SB_WIDTH ** -0.5),
        'w_proj_dsa': nrm(ks[12], (DEPTH, DSA_WIDTH, D_MODEL), DSA_WIDTH ** -0.5),
        'w_out': nrm(ks[13], (DEPTH, D_MODEL, D_MODEL), D_MODEL ** -0.5),
        'norm_ffn': 1.0 + nrm(ks[14], (DEPTH, D_MODEL), 0.01),
        'w_up': nrm(ks[15], (DEPTH, D_MODEL, 2 * D_FF), D_MODEL ** -0.5),
        'conv_w': nrm(ks[16], (DEPTH, CONV_W, 2 * D_FF), CONV_W ** -0.5),
        'conv_b': nrm(ks[17], (DEPTH, 2 * D_FF), 0.01),
        'w_down': nrm(ks[18], (DEPTH, D_FF, D_MODEL), D_FF ** -0.5),
        'norm_final': 1.0 + nrm(ks[19], (D_MODEL,), 0.01),
    }


def reference(x_prompt, x_sample, cache_k_sb, cache_v_sb, cache_k_dsa, cache_v_dsa, cache_k_idx,
              state_conv, page_table, norm_mix, w_in, w_proj_sb, w_proj_dsa, w_out, norm_ffn,
              w_up, conv_w, conv_b, w_down, norm_final):
    b_p, s_p = x_prompt.shape[:2]
    t_s = x_sample.shape[1]
    past = page_table.shape[1] * PAGE_SIZE
    pos_p = jnp.arange(s_p)
    pos_s = past + jnp.arange(t_s)
    names = ('k_sb_p', 'v_sb_p', 'k_dsa_p', 'v_dsa_p', 'k_idx_p', 'conv_p',
             'k_sb_s', 'v_sb_s', 'k_dsa_s', 'v_dsa_s', 'k_idx_s', 'conv_s')
    st = {n: [] for n in names}
    xp, xs = x_prompt, x_sample
    for l in range(DEPTH):
        pp = _project(_rmsnorm(xp, norm_mix[l]), w_in[l], pos_p)
        ps = _project(_rmsnorm(xs, norm_mix[l]), w_in[l], pos_s)
        o_sb_p = _sb_prompt(pp['q_sb'], pp['k_sb'], pp['v_sb'])
        o_d_p = _dsa_prompt(pp['q_d'], pp['k_d'], pp['v_d'], pp['q_i'], pp['k_i'], pp['w_i'])
        o_sb_s = _sb_sample(ps['q_sb'], ps['k_sb'], ps['v_sb'], cache_k_sb[l], cache_v_sb[l], page_table)
        o_d_s = _dsa_sample(ps['q_d'], ps['k_d'], ps['v_d'], ps['q_i'], ps['k_i'], ps['w_i'],
                            cache_k_dsa[l], cache_v_dsa[l], cache_k_idx[l], page_table)
        xp = xp + _merge(o_sb_p, o_d_p, pp['g_sb'], pp['g_d'], w_proj_sb[l], w_proj_dsa[l], w_out[l])
        xs = xs + _merge(o_sb_s, o_d_s, ps['g_sb'], ps['g_d'], w_proj_sb[l], w_proj_dsa[l], w_out[l])
        conv0 = jnp.zeros((b_p, CONV_W - 1, 2 * D_FF), xp.dtype)
        f_p, cv_p = _conv_ffn(_rmsnorm(xp, norm_ffn[l]), conv0, w_up[l], conv_w[l], conv_b[l], w_down[l])
        f_s, cv_s = _conv_ffn(_rmsnorm(xs, norm_ffn[l]), state_conv[l], w_up[l], conv_w[l], conv_b[l], w_down[l])
        xp = xp + f_p
        xs = xs + f_s
        for n, val in (('k_sb_p', pp['k_sb']), ('v_sb_p', pp['v_sb']), ('k_dsa_p', pp['k_d']),
                       ('v_dsa_p', pp['v_d']), ('k_idx_p', pp['k_i']), ('conv_p', cv_p),
                       ('k_sb_s', ps['k_sb']), ('v_sb_s', ps['v_sb']), ('k_dsa_s', ps['k_d']),
                       ('v_dsa_s', ps['v_d']), ('k_idx_s', ps['k_i']), ('conv_s', cv_s)):
            st[n].append(val)
    y_prompt = _rmsnorm(xp, norm_final)
    y_sample = _rmsnorm(xs, norm_final)
    return (y_prompt, y_sample,
            jnp.stack(st['k_sb_p']), jnp.stack(st['v_sb_p']), jnp.stack(st['k_dsa_p']),
            jnp.stack(st['v_dsa_p']), jnp.stack(st['k_idx_p']), jnp.stack(st['conv_p']),
            jnp.stack(st['k_sb_s']), jnp.stack(st['v_sb_s']), jnp.stack(st['k_dsa_s']),
            jnp.stack(st['v_dsa_s']), jnp.stack(st['k_idx_s']), jnp.stack(st['conv_s']))
```

```python
import functools

import jax
import jax.numpy as jnp
from jax import lax
from jax.experimental import pallas as pl
from jax.experimental.pallas import tpu as pltpu

F32 = jnp.float32
BF16 = jnp.bfloat16
I32 = jnp.int32

HEAD_DIM = 128
IDX_DIM = 128
PAGE_SIZE = 128
TOPK_MAX = 256
ROPE_THETA = 10000.0
NORM_EPS = 1e-6
LANES = 128
HEAD_ROWS = 16
NEG_BIG = -1e30
INT_MIN = -2 ** 31
VMEM_LIMIT = 56 * 1024 * 1024


def _cparams(sem):
    return pltpu.CompilerParams(dimension_semantics=sem, vmem_limit_bytes=VMEM_LIMIT)


def _rope_apply(x, cos2, sin2):
    return x * cos2 + pltpu.roll(x, HEAD_DIM // 2, axis=1) * sin2


def _softplus(z):
    return jnp.maximum(z, 0.0) + jnp.log(1.0 + jnp.exp(-jnp.abs(z)))


def _split_dot(a, m):
    hi = a.astype(BF16)
    lo = (a - hi.astype(F32)).astype(BF16)
    return (jnp.dot(hi, m, preferred_element_type=F32) + jnp.dot(lo, m, preferred_element_type=F32))


def _proj_body(*refs, segs, n_out, has_rope, wi_scale):
    x_ref, g_ref, w_ref = refs[:3]
    pos = 3
    if has_rope:
        cos_ref, sin_ref = refs[3:5]
        pos = 5
    out_refs = refs[pos:pos + n_out]
    h_ref = refs[pos + n_out]
    j = pl.program_id(1)

    @pl.when(j == 0)
    def _norm():
        x = x_ref[...]
        ms = jnp.mean(x * x, axis=-1, keepdims=True)
        h_ref[...] = ((x * lax.rsqrt(ms + NORM_EPS)) * g_ref[...]).astype(BF16)

    acc = jnp.dot(h_ref[...], w_ref[...], preferred_element_type=F32)
    tn = acc.shape[1]
    for s, (kind, oidx) in enumerate(segs):
        outs = [out_refs[o] for o in oidx]

        @pl.when(j == s)
        def _epilogue(kind=kind, outs=outs):
            if kind == "plain":
                for o in outs:
                    o[...] = acc.astype(o.dtype)
            elif kind == "sigmoid":
                y = jax.nn.sigmoid(acc)
                for o in outs:
                    o[...] = y.astype(o.dtype)
            elif kind == "rope":
                cos2, sin2 = cos_ref[...], sin_ref[...]
                for c in range(tn // HEAD_DIM):
                    sl = slice(c * HEAD_DIM, (c + 1) * HEAD_DIM)
                    y = _rope_apply(acc[:, sl], cos2, sin2)
                    for o in outs:
                        o[:, sl] = y.astype(o.dtype)
            elif kind == "kiwi":
                y = _rope_apply(acc[:, :IDX_DIM], cos_ref[...], sin_ref[...])
                outs[0][...] = y
                outs[1][...] = y.astype(BF16)
                outs[2][...] = acc[:, IDX_DIM:] * wi_scale
            else:
                raise ValueError(kind)


def _norm_proj(x, gain, w, tn, segs, outs, rope=None, wi_scale=1.0, tm=512):
    m, k = x.shape
    tm = min(tm, m)
    n_seg = len(segs)
    assert w.shape == (k, n_seg * tn) and m % tm == 0
    in_specs = [pl.BlockSpec((tm, k), lambda i, j: (i, 0)),
                pl.BlockSpec((1, k), lambda i, j: (0, 0)),
                pl.BlockSpec((k, tn), lambda i, j: (0, j))]
    args = [x, gain, w]
    if rope is not None:
        in_specs += [pl.BlockSpec((tm, HEAD_DIM), lambda i, j: (i, 0))] * 2
        args += list(rope)
    out_specs, out_shapes = [], []
    for dt, start, cnt in outs:
        width = tn if segs[start][0] != "kiwi" else IDX_DIM
        out_specs.append(pl.BlockSpec(
            (tm, width), lambda i, j, start=start, cnt=cnt: (i, jnp.clip(j - start, 0, cnt - 1))))
        out_shapes.append(jax.ShapeDtypeStruct((m, width * cnt), dt))
    body = functools.partial(_proj_body, segs=segs, n_out=len(outs), has_rope=rope is not None, wi_scale=wi_scale)
    return pl.pallas_call(
        body, grid=(m // tm, n_seg), in_specs=in_specs, out_specs=out_specs, out_shape=out_shapes,
        scratch_shapes=[pltpu.VMEM((tm, k), BF16)],
        compiler_params=_cparams(("arbitrary", "arbitrary")))(*args)


def _sb_prompt_body(q_ref, k_ref, v_ref, tri_ref, o_ref, *, tq, scale):
    i = pl.program_id(1)
    q = q_ref[...]
    tri = tri_ref[...]

    def block(kb, acc, carry, masked):
        start = pl.multiple_of(kb * tq, tq)
        k = k_ref[pl.ds(start, tq), :]
        v = v_ref[pl.ds(start, tq), :]
        z = lax.dot_general(q, k, (((1,), (1,)), ((), ())), preferred_element_type=F32) * scale
        lg = -_softplus(z)
        if masked:
            row = lax.broadcasted_iota(I32, (tq, tq), 0)
            col = lax.broadcasted_iota(I32, (tq, tq), 1)
            keep = col < row
            lg = jnp.where(keep, lg, 0.0)
        suffix = _split_dot(lg, tri)
        w = jnp.exp(z + lg + suffix + carry)
        if masked:
            w = jnp.where(keep, w, 0.0)
        acc = acc + jnp.dot(w.astype(BF16), v, preferred_element_type=F32)
        carry = carry + jnp.sum(lg, axis=1, keepdims=True)
        return acc, carry

    acc, carry = block(i, jnp.zeros((tq, HEAD_DIM), F32), jnp.zeros((tq, 1), F32), True)

    def step(s, c):
        return block(i - 1 - s, c[0], c[1], False)

    acc, carry = lax.fori_loop(0, i, step, (acc, carry))
    o_ref[...] = acc.astype(o_ref.dtype)


def _sb_prompt(q, k, v, n_heads, tq=256):
    s = q.shape[0]
    tq = min(tq, s)
    tri = (lax.broadcasted_iota(I32, (tq, tq), 0) > lax.broadcasted_iota(I32, (tq, tq), 1)).astype(BF16)
    body = functools.partial(_sb_prompt_body, tq=tq, scale=HEAD_DIM ** -0.5)
    return pl.pallas_call(
        body, grid=(n_heads, s // tq),
        in_specs=[pl.BlockSpec((tq, HEAD_DIM), lambda h, i: (i, h)),
                  pl.BlockSpec((s, HEAD_DIM), lambda h, i: (0, h)),
                  pl.BlockSpec((s, HEAD_DIM), lambda h, i: (0, h)),
                  pl.BlockSpec((tq, tq), lambda h, i: (0, 0))],
        out_specs=pl.BlockSpec((tq, HEAD_DIM), lambda h, i: (i, h)),
        out_shape=jax.ShapeDtypeStruct((s, n_heads * HEAD_DIM), BF16),
        compiler_params=_cparams(("arbitrary", "arbitrary")))(q, k, v, tri)


def _sort_key(score):
    b = lax.bitcast_convert_type(score, I32)
    return jnp.where(b < 0, (b ^ jnp.int32(0x7FFFFFFF)) + 1, b)


def _topk_threshold(count_ge, rows, topk):
    def bit_step(s, t):
        cand = t + lax.shift_left(jnp.int32(1), jnp.int32(31) - s)
        return jnp.where(count_ge(cand) >= float(topk), cand, t)

    t = lax.fori_loop(0, 32, bit_step, jnp.full((rows, 1), INT_MIN, I32))
    return jnp.maximum(t, INT_MIN + 1)


def _dsa_prompt_body(qi_ref, ki_ref, wi_ref, q_ref, k_ref, v_ref, o_ref, key_ref, bias_ref, wb_ref,
                     *, tq, tk, n_idx_heads, topk, scale):
    i = pl.program_id(0)
    h = pl.program_id(1)
    n_chunks = (i * tq + tq - 1) // tk + 1
    qpos = i * tq + lax.broadcasted_iota(I32, (tq, tk), 0)

    @pl.when(h == 0)
    def _select():
        for hh in range(n_idx_heads):
            wb_ref[hh] = jnp.broadcast_to(wi_ref[:, hh:hh + 1], (tq, LANES))

        def score_chunk(c, _):
            start = pl.multiple_of(c * tk, tk)
            ki = ki_ref[pl.ds(start, tk), :]
            acc = jnp.zeros((tq, tk), F32)
            for hh in range(n_idx_heads):
                s = lax.dot_general(qi_ref[:, hh * IDX_DIM:(hh + 1) * IDX_DIM], ki, (((1,), (1,)), ((), ())),
                                    preferred_element_type=F32)
                wb = wb_ref[hh]
                acc = acc + jnp.maximum(s, 0.0) * jnp.concatenate([wb] * (tk // LANES), axis=1)
            kpos = start + lax.broadcasted_iota(I32, (tq, tk), 1)
            key_ref[c] = jnp.where(kpos <= qpos, _sort_key(acc), INT_MIN)
            return 0

        lax.fori_loop(0, n_chunks, score_chunk, 0)

        def count_ge(cand):
            def cnt_chunk(c, part):
                ge = (key_ref[c] >= cand).astype(I32)
                for u in range(tk // LANES):
                    part = part + ge[:, u * LANES:(u + 1) * LANES]
                return part

            part = lax.fori_loop(0, n_chunks, cnt_chunk, jnp.zeros((tq, LANES), I32))
            return jnp.sum(part.astype(F32), axis=1, keepdims=True)

        thr = _topk_threshold(count_ge, tq, topk)

        def bias_chunk(c, _):
            bias_ref[c] = jnp.where(key_ref[c] >= thr, 0.0, NEG_BIG)
            return 0

        lax.fori_loop(0, n_chunks, bias_chunk, 0)

    q = q_ref[...]

    def attn_chunk(c, carry):
        m, l, acc = carry
        start = pl.multiple_of(c * tk, tk)
        k = k_ref[pl.ds(start, tk), :]
        v = v_ref[pl.ds(start, tk), :]
        s = lax.dot_general(q, k, (((1,), (1,)), ((), ())), preferred_element_type=F32) * scale
        s = s + bias_ref[c]
        m_new = jnp.maximum(m, jnp.max(s, axis=1, keepdims=True))
        alpha = jnp.exp(m - m_new)
        p = jnp.exp(s - m_new)
        l = alpha * l + jnp.sum(p, axis=1, keepdims=True)
        acc = alpha * acc + jnp.dot(p.astype(BF16), v, preferred_element_type=F32)
        return m_new, l, acc

    m, l, acc = lax.fori_loop(
        0, n_chunks, attn_chunk,
        (jnp.full((tq, 1), -jnp.inf, F32), jnp.zeros((tq, 1), F32), jnp.zeros((tq, HEAD_DIM), F32)))
    o_ref[...] = (acc / l).astype(o_ref.dtype)


def _dsa_prompt(qi, ki, wi, q, k, v, n_heads, n_idx_heads, topk, tq=256, tk=512):
    s = q.shape[0]
    tq = min(tq, s)
    tk = min(tk, s)
    body = functools.partial(_dsa_prompt_body, tq=tq, tk=tk, n_idx_heads=n_idx_heads, topk=topk,
                             scale=HEAD_DIM ** -0.5)
    return pl.pallas_call(
        body, grid=(s // tq, n_heads),
        in_specs=[pl.BlockSpec((tq, n_idx_heads * IDX_DIM), lambda i, h: (i, 0)),
                  pl.BlockSpec((s, IDX_DIM), lambda i, h: (0, 0)),
                  pl.BlockSpec((tq, LANES), lambda i, h: (i, 0)),
                  pl.BlockSpec((tq, HEAD_DIM), lambda i, h: (i, h)),
                  pl.BlockSpec((s, HEAD_DIM), lambda i, h: (0, h)),
                  pl.BlockSpec((s, HEAD_DIM), lambda i, h: (0, h))],
        out_specs=pl.BlockSpec((tq, HEAD_DIM), lambda i, h: (i, h)),
        out_shape=jax.ShapeDtypeStruct((s, n_heads * HEAD_DIM), BF16),
        scratch_shapes=[pltpu.VMEM((s // tk, tq, tk), I32), pltpu.VMEM((s // tk, tq, tk), F32),
                        pltpu.VMEM((n_idx_heads, tq, LANES), F32)],
        compiler_params=_cparams(("arbitrary", "arbitrary")))(qi, ki, wi, q, k, v)


def _head_block_mask(width):
    row = lax.broadcasted_iota(I32, (HEAD_ROWS, width), 0)
    col = lax.broadcasted_iota(I32, (HEAD_ROWS, width), 1)
    return row == col // HEAD_DIM


def _paged_attn_body(*refs, mode, pp, scale):
    pt_ref = refs[0]
    del pt_ref
    q_ref = refs[1]
    pos = 2
    if mode == "dsa":
        bias_ref, bnew_ref, knew_ref, vnew_ref = refs[2:6]
        pos = 6
    else:
        tri_ref = refs[2]
        pos = 3
    k_refs = refs[pos:pos + pp]
    v_refs = refs[pos + pp:pos + 2 * pp]
    o_ref = refs[pos + 2 * pp]
    acc_ref, m_ref, l_ref = refs[pos + 2 * pp + 1:]
    p = pl.program_id(1)
    width = q_ref.shape[-1]
    blk = _head_block_mask(width)
    qf = jnp.where(blk, jnp.broadcast_to(q_ref[...], (HEAD_ROWS, width)), 0.0)
    qb = qf.astype(BF16)

    @pl.when(p == 0)
    def _init():
        acc_ref[...] = jnp.zeros_like(acc_ref)
        l_ref[...] = jnp.zeros_like(l_ref)
        m_ref[...] = jnp.zeros_like(m_ref) if mode == "sb" else jnp.full_like(m_ref, -jnp.inf)

    for r in range(pp):
        k = k_refs[r][...].astype(BF16)
        v = v_refs[r][...].astype(BF16)
        z = lax.dot_general(qb, k, (((1,), (1,)), ((), ())), preferred_element_type=F32) * scale
        if mode == "sb":
            lg = -_softplus(z)
            suffix = _split_dot(lg, tri_ref[...])
            w = jnp.exp(z + lg + suffix + m_ref[...])
            acc_ref[...] += jnp.dot(w.astype(BF16), v, preferred_element_type=F32)
            m_ref[...] += jnp.sum(lg, axis=1, keepdims=True)
        else:
            s = z + bias_ref[:, r * PAGE_SIZE:(r + 1) * PAGE_SIZE]
            m_old = m_ref[...]
            m_new = jnp.maximum(m_old, jnp.max(s, axis=1, keepdims=True))
            alpha = jnp.exp(m_old - m_new)
            w = jnp.exp(s - m_new)
            l_ref[...] = alpha * l_ref[...] + jnp.sum(w, axis=1, keepdims=True)
            acc_ref[...] = alpha * acc_ref[...] + jnp.dot(w.astype(BF16), v, preferred_element_type=F32)
            m_ref[...] = m_new

    @pl.when(p == pl.num_programs(1) - 1)
    def _finish():
        acc = acc_ref[...]
        if mode == "dsa":
            s_new = jnp.sum(qf * knew_ref[...], axis=1, keepdims=True) * scale + bnew_ref[:, 0:1]
            m_old = m_ref[...]
            m_new = jnp.maximum(m_old, s_new)
            alpha = jnp.exp(m_old - m_new)
            w_new = jnp.exp(s_new - m_new)
            l = alpha * l_ref[...] + w_new
            acc = (alpha * acc + w_new * vnew_ref[...]) / l
        o_ref[...] = jnp.sum(jnp.where(blk, acc, 0.0), axis=0, keepdims=True)


def _paged_attn(mode, q, cache_k, cache_v, page_table, *, bias=None, bias_new=None, k_new=None, v_new=None, pp=8):
    b, _, width = q.shape
    n_pages = page_table.shape[1]
    pp = min(pp, n_pages)
    assert n_pages % pp == 0
    steps = n_pages // pp
    newest_first = mode == "sb"

    def page_map(r):
        def index_map(bi, p, pt):
            page = (n_pages - 1 - (p * pp + r)) if newest_first else (p * pp + r)
            return (pt[bi, page], 0, 0)
        return index_map

    row_spec = pl.BlockSpec((None, 1, width), lambda bi, p, pt: (bi, 0, 0))
    in_specs = [row_spec]
    args = [q]
    if mode == "dsa":
        in_specs += [pl.BlockSpec((None, 1, pp * PAGE_SIZE), lambda bi, p, pt: (bi, 0, p)),
                     pl.BlockSpec((None, 1, LANES), lambda bi, p, pt: (bi, 0, 0)), row_spec, row_spec]
        args += [bias, bias_new, k_new, v_new]
    else:
        tri = (lax.broadcasted_iota(I32, (PAGE_SIZE, PAGE_SIZE), 0)
               > lax.broadcasted_iota(I32, (PAGE_SIZE, PAGE_SIZE), 1)).astype(BF16)
        in_specs += [pl.BlockSpec((PAGE_SIZE, PAGE_SIZE), lambda bi, p, pt: (0, 0))]
        args += [tri]
    page_specs = [pl.BlockSpec((None, PAGE_SIZE, width), page_map(r)) for r in range(pp)]
    in_specs += page_specs + page_specs
    args += [cache_k] * pp + [cache_v] * pp
    grid_spec = pltpu.PrefetchScalarGridSpec(
        num_scalar_prefetch=1, grid=(b, steps), in_specs=in_specs, out_specs=row_spec,
        scratch_shapes=[pltpu.VMEM((HEAD_ROWS, width), F32), pltpu.VMEM((HEAD_ROWS, 1), F32),
                        pltpu.VMEM((HEAD_ROWS, 1), F32)])
    body = functools.partial(_paged_attn_body, mode=mode, pp=pp, scale=HEAD_DIM ** -0.5)
    return pl.pallas_call(
        body, grid_spec=grid_spec, out_shape=jax.ShapeDtypeStruct((b, 1, width), F32),
        compiler_params=_cparams(("arbitrary", "arbitrary")))(page_table, *args)


def _idx_scores_body(pt_ref, qi_ref, wcol_ref, knew_ref, *refs, pp):
    del pt_ref
    page_refs = refs[:pp]
    o_ref, onew_ref = refs[pp:pp + 2]
    qi = qi_ref[...]
    qb = qi.astype(BF16)
    wcol = wcol_ref[...]
    for r in range(pp):
        kp = page_refs[r][...].astype(BF16)
        s = lax.dot_general(qb, kp, (((1,), (1,)), ((), ())), preferred_element_type=F32)
        o_ref[:, r * PAGE_SIZE:(r + 1) * PAGE_SIZE] = jnp.sum(jnp.maximum(s, 0.0) * wcol, axis=0, keepdims=True)

    @pl.when(pl.program_id(1) == 0)
    def _new_key():
        s = jnp.sum(qi * knew_ref[...], axis=1, keepdims=True)
        sc = jnp.sum(jnp.maximum(s, 0.0) * wcol[:, 0:1], axis=0, keepdims=True)
        lane = lax.broadcasted_iota(I32, (1, LANES), 1)
        onew_ref[...] = jnp.where(lane == 0, sc, -jnp.inf)


def _idx_scores_sample(qi, wcol, ki_new, cache_ki, page_table, pp=8):
    b, hi, e = qi.shape
    n_pages = page_table.shape[1]
    pp = min(pp, n_pages)
    steps = n_pages // pp

    def page_map(r):
        return lambda bi, p, pt: (pt[bi, p * pp + r], 0, 0)

    grid_spec = pltpu.PrefetchScalarGridSpec(
        num_scalar_prefetch=1, grid=(b, steps),
        in_specs=[pl.BlockSpec((None, hi, e), lambda bi, p, pt: (bi, 0, 0)),
                  pl.BlockSpec((None, hi, LANES), lambda bi, p, pt: (bi, 0, 0)),
                  pl.BlockSpec((None, 1, e), lambda bi, p, pt: (bi, 0, 0))]
        + [pl.BlockSpec((None, PAGE_SIZE, e), page_map(r)) for r in range(pp)],
        out_specs=[pl.BlockSpec((None, 1, pp * PAGE_SIZE), lambda bi, p, pt: (bi, 0, p)),
                   pl.BlockSpec((None, 1, LANES), lambda bi, p, pt: (bi, 0, 0))])
    return pl.pallas_call(
        functools.partial(_idx_scores_body, pp=pp), grid_spec=grid_spec,
        out_shape=[jax.ShapeDtypeStruct((b, 1, n_pages * PAGE_SIZE), F32), jax.ShapeDtypeStruct((b, 1, LANES), F32)],
        compiler_params=_cparams(("arbitrary", "arbitrary")))(page_table, qi, wcol, ki_new, *([cache_ki] * pp))


def _topk_bias_body(s_ref, o_ref, *, topk):
    keys = _sort_key(s_ref[...])
    keys = jnp.where(s_ref[...] == -jnp.inf, INT_MIN, keys)

    def count_ge(cand):
        return jnp.sum((keys >= cand).astype(F32), axis=1, keepdims=True)

    thr = _topk_threshold(count_ge, keys.shape[0], topk)
    o_ref[...] = jnp.where(keys >= thr, 0.0, NEG_BIG)


def _topk_bias(scores, topk):
    return pl.pallas_call(
        functools.partial(_topk_bias_body, topk=topk),
        out_shape=jax.ShapeDtypeStruct(scores.shape, F32))(scores)


def _merge_gate_body(osb_ref, od_ref, wsb_ref, wd_ref, gsb_ref, gd_ref, o_ref):
    y_sb = jnp.dot(osb_ref[...], wsb_ref[...], preferred_element_type=F32)
    y_d = jnp.dot(od_ref[...], wd_ref[...], preferred_element_type=F32)
    o_ref[...] = (gsb_ref[...] * y_sb + gd_ref[...] * y_d).astype(o_ref.dtype)


def _merge_gate(o_sb, o_d, w_sb, w_d, gates, tm=512, tn=1024):
    m, ksb = o_sb.shape
    kd = o_d.shape[1]
    d = w_sb.shape[1]
    tm = min(tm, m)
    tn = min(tn, d)
    nj = d // tn
    return pl.pallas_call(
        _merge_gate_body, grid=(m // tm, nj),
        in_specs=[pl.BlockSpec((tm, ksb), lambda i, j: (i, 0)),
                  pl.BlockSpec((tm, kd), lambda i, j: (i, 0)),
                  pl.BlockSpec((ksb, tn), lambda i, j: (0, j)),
                  pl.BlockSpec((kd, tn), lambda i, j: (0, j)),
                  pl.BlockSpec((tm, tn), lambda i, j: (i, j)),
                  pl.BlockSpec((tm, tn), lambda i, j: (i, nj + j))],
        out_specs=pl.BlockSpec((tm, tn), lambda i, j: (i, j)),
        out_shape=jax.ShapeDtypeStruct((m, d), BF16),
        compiler_params=_cparams(("arbitrary", "arbitrary")))(o_sb, o_d, w_sb, w_d, gates, gates)


def _matmul_res_body(a_ref, w_ref, r_ref, g_ref, *refs, emit_sum, emit_norm):
    outs = refs[:-1]
    acc_ref = refs[-1]
    kk = pl.program_id(1)

    @pl.when(kk == 0)
    def _init():
        acc_ref[...] = jnp.zeros_like(acc_ref)

    acc_ref[...] += jnp.dot(a_ref[...], w_ref[...], preferred_element_type=F32)

    @pl.when(kk == pl.num_programs(1) - 1)
    def _finish():
        y = r_ref[...] + acc_ref[...]
        o = 0
        if emit_sum:
            outs[o][...] = y
            o += 1
        if emit_norm:
            ms = jnp.mean(y * y, axis=-1, keepdims=True)
            outs[o][...] = ((y * lax.rsqrt(ms + NORM_EPS)) * g_ref[...]).astype(outs[o].dtype)


def _matmul_res(a, w, res, gain, *, emit_sum, norm_dtype, tm=512, tk=None):
    m, k = a.shape
    n = w.shape[1]
    tm = min(tm, m)
    tk = k if tk is None else tk
    assert k % tk == 0
    out_shapes, out_specs = [], []
    if emit_sum:
        out_shapes.append(jax.ShapeDtypeStruct((m, n), F32))
    if norm_dtype is not None:
        out_shapes.append(jax.ShapeDtypeStruct((m, n), norm_dtype))
    out_specs = [pl.BlockSpec((tm, n), lambda i, kk: (i, 0)) for _ in out_shapes]
    body = functools.partial(_matmul_res_body, emit_sum=emit_sum, emit_norm=norm_dtype is not None)
    return pl.pallas_call(
        body, grid=(m // tm, k // tk),
        in_specs=[pl.BlockSpec((tm, tk), lambda i, kk: (i, kk)),
                  pl.BlockSpec((tk, n), lambda i, kk: (kk, 0)),
                  pl.BlockSpec((tm, n), lambda i, kk: (i, 0)),
                  pl.BlockSpec((1, n), lambda i, kk: (0, 0))],
        out_specs=out_specs, out_shape=out_shapes,
        scratch_shapes=[pltpu.VMEM((tm, n), F32)],
        compiler_params=_cparams(("arbitrary", "arbitrary")))(a, w, res, gain)


def _silu_gate(cg, cv):
    return (cg * jax.nn.sigmoid(cg)) * cv


def _ffn_up_prompt_body(h_ref, wg_ref, wv_ref, cwg_ref, cwv_ref, cbg_ref, cbv_ref, pg_ref, pv_ref,
                        a_ref, og_ref, ov_ref, carry_g, carry_v, *, tm):
    mi = pl.program_id(1)
    h = h_ref[...]
    row = lax.broadcasted_iota(I32, (tm, 1), 0)

    @pl.when(mi == 0)
    def _init():
        carry_g[0:2, :] = pg_ref[...]
        carry_v[0:2, :] = pv_ref[...]

    def conv(w_ref, cw_ref, cb_ref, carry):
        u = jnp.dot(h, w_ref[...], preferred_element_type=F32)
        c_prev, c_last = carry[0:1, :], carry[1:2, :]
        u1 = jnp.where(row == 0, c_last, pltpu.roll(u, 1, axis=0))
        u2 = jnp.where(row == 0, c_prev, jnp.where(row == 1, c_last, pltpu.roll(u, 2, axis=0)))
        carry[0:2, :] = u[tm - 2:tm, :]
        return cb_ref[...] + u2 * cw_ref[0:1, :] + u1 * cw_ref[1:2, :] + u * cw_ref[2:3, :]

    cg = conv(wg_ref, cwg_ref, cbg_ref, carry_g)
    cv = conv(wv_ref, cwv_ref, cbv_ref, carry_v)
    a_ref[...] = _silu_gate(cg, cv).astype(a_ref.dtype)

    @pl.when(mi == pl.num_programs(1) - 1)
    def _state():
        og_ref[...] = carry_g[0:2, :]
        ov_ref[...] = carry_v[0:2, :]


def _ffn_up_prompt(h, w_up, conv_w, conv_b, conv_prev, tm=512, tf=512):
    m, d = h.shape
    f = w_up.shape[1] // 2
    tm = min(tm, m)
    tf = min(tf, f)
    nf = f // tf
    assert f % tf == 0 and m % tm == 0
    lo = lambda fi, mi: (0, fi)
    hi = lambda fi, mi: (0, nf + fi)
    body = functools.partial(_ffn_up_prompt_body, tm=tm)
    return pl.pallas_call(
        body, grid=(nf, m // tm),
        in_specs=[pl.BlockSpec((tm, d), lambda fi, mi: (mi, 0)),
                  pl.BlockSpec((d, tf), lo), pl.BlockSpec((d, tf), hi),
                  pl.BlockSpec((3, tf), lo), pl.BlockSpec((3, tf), hi),
                  pl.BlockSpec((1, tf), lo), pl.BlockSpec((1, tf), hi),
                  pl.BlockSpec((2, tf), lo), pl.BlockSpec((2, tf), hi)],
        out_specs=[pl.BlockSpec((tm, tf), lambda fi, mi: (mi, fi)),
                   pl.BlockSpec((2, tf), lo), pl.BlockSpec((2, tf), lo)],
        out_shape=[jax.ShapeDtypeStruct((m, f), BF16), jax.ShapeDtypeStruct((2, f), F32),
                   jax.ShapeDtypeStruct((2, f), F32)],
        scratch_shapes=[pltpu.VMEM((8, tf), F32), pltpu.VMEM((8, tf), F32)],
        compiler_params=_cparams(("arbitrary", "arbitrary")))(
            h, w_up, w_up, conv_w, conv_w, conv_b, conv_b, conv_prev, conv_prev)


def _ffn_up_sample_body(h_ref, wg_ref, wv_ref, cwg_ref, cwv_ref, cbg_ref, cbv_ref,
                        s0g_ref, s0v_ref, s1g_ref, s1v_ref, a_ref, ug_ref, uv_ref):
    h = h_ref[...]

    def conv(w_ref, cw_ref, cb_ref, s0_ref, s1_ref, u_ref):
        u = jnp.dot(h, w_ref[...], preferred_element_type=F32)
        u_ref[...] = u
        return cb_ref[...] + s0_ref[...] * cw_ref[0:1, :] + s1_ref[...] * cw_ref[1:2, :] + u * cw_ref[2:3, :]

    cg = conv(wg_ref, cwg_ref, cbg_ref, s0g_ref, s1g_ref, ug_ref)
    cv = conv(wv_ref, cwv_ref, cbv_ref, s0v_ref, s1v_ref, uv_ref)
    a_ref[...] = _silu_gate(cg, cv).astype(a_ref.dtype)


def _ffn_up_sample(h, w_up, conv_w, conv_b, state, tf=512):
    b, d = h.shape
    f = w_up.shape[1] // 2
    tf = min(tf, f)
    nf = f // tf
    col = lambda off: (lambda fi: (0, off * nf + fi))
    wspec = lambda off: pl.BlockSpec((d, tf), col(off))
    sspec = lambda off: pl.BlockSpec((b, tf), col(off))
    return pl.pallas_call(
        _ffn_up_sample_body, grid=(nf,),
        in_specs=[pl.BlockSpec((b, d), lambda fi: (0, 0)), wspec(0), wspec(1),
                  pl.BlockSpec((3, tf), col(0)), pl.BlockSpec((3, tf), col(1)),
                  pl.BlockSpec((1, tf), col(0)), pl.BlockSpec((1, tf), col(1)),
                  sspec(0), sspec(1), sspec(2), sspec(3)],
        out_specs=[sspec(0), sspec(0), sspec(0)],
        out_shape=[jax.ShapeDtypeStruct((b, f), BF16), jax.ShapeDtypeStruct((b, f), F32),
                   jax.ShapeDtypeStruct((b, f), F32)],
        compiler_params=_cparams(("arbitrary",)))(
            h, w_up, w_up, conv_w, conv_w, conv_b, conv_b, state, state, state, state)


def _rope_tables(pos, dim):
    inv = ROPE_THETA ** (-jnp.arange(0, dim, 2, dtype=F32) / dim)
    ang = pos.astype(F32)[:, None] * inv[None, :]
    cos, sin = jnp.cos(ang), jnp.sin(ang)
    return jnp.concatenate([cos, cos], axis=1), jnp.concatenate([-sin, sin], axis=1)


def _project_group(x, gain, wts, rope, n_idx_heads, q_dtype):
    w_sb, w_dsa, w_qi, w_kiwi, w_gate = wts
    tn = w_sb.shape[1] // 3
    q_sb, k_sb, k_sb_b, v_sb, v_sb_b = _norm_proj(
        x, gain, w_sb, tn,
        segs=[("plain", [0]), ("plain", [1, 2]), ("plain", [3, 4])],
        outs=[(q_dtype, 0, 1), (F32, 1, 1), (BF16, 1, 1), (F32, 2, 1), (BF16, 2, 1)])
    q_d, k_d, k_d_b, v_d, v_d_b = _norm_proj(
        x, gain, w_dsa, tn,
        segs=[("rope", [0]), ("rope", [1, 2]), ("plain", [3, 4])],
        outs=[(q_dtype, 0, 1), (F32, 1, 1), (BF16, 1, 1), (F32, 2, 1), (BF16, 2, 1)], rope=rope)
    n_qi = w_qi.shape[1] // tn
    (q_i,) = _norm_proj(x, gain, w_qi, tn, segs=[("rope", [0])] * n_qi, outs=[(q_dtype, 0, n_qi)], rope=rope)
    k_i, k_i_b, w_i = _norm_proj(
        x, gain, w_kiwi, 2 * IDX_DIM, segs=[("kiwi", [0, 1, 2])],
        outs=[(F32, 0, 1), (BF16, 0, 1), (F32, 0, 1)], rope=rope,
        wi_scale=n_idx_heads ** -0.5 * IDX_DIM ** -0.5)
    n_g = w_gate.shape[1] // tn
    (gates,) = _norm_proj(x, gain, w_gate, tn, segs=[("sigmoid", [0])] * n_g, outs=[(F32, 0, n_g)])
    return dict(q_sb=q_sb, k_sb=k_sb, k_sb_b=k_sb_b, v_sb=v_sb, v_sb_b=v_sb_b, q_d=q_d, k_d=k_d, k_d_b=k_d_b,
                v_d=v_d, v_d_b=v_d_b, q_i=q_i, k_i=k_i, k_i_b=k_i_b, w_i=w_i, gates=gates)


def _post_attention(x, o_sb, o_d, gates, w_proj_sb, w_proj_dsa, w_out, norm_ffn):
    merged = _merge_gate(o_sb, o_d, w_proj_sb, w_proj_dsa, gates)
    return _matmul_res(merged, w_out, x, norm_ffn, emit_sum=True, norm_dtype=BF16)


def kernel(x_prompt, x_sample, cache_k_sb, cache_v_sb, cache_k_dsa, cache_v_dsa, cache_k_idx, state_conv,
           page_table, norm_mix, w_in, w_proj_sb, w_proj_dsa, w_out, norm_ffn, w_up, conv_w, conv_b, w_down,
           norm_final):
    b_p, s_p, d_model = x_prompt.shape
    b_s, t_s, _ = x_sample.shape
    depth = w_in.shape[0]
    assert b_p == 1 and t_s == 1 and depth == 1
    n_pool, page, sb_heads, hd = cache_k_sb.shape[1:]
    dsa_heads = cache_k_dsa.shape[3]
    assert hd == HEAD_DIM and page == PAGE_SIZE and cache_k_idx.shape[-1] == IDX_DIM
    sb_w, dsa_w = sb_heads * HEAD_DIM, dsa_heads * HEAD_DIM
    d_ff = w_down.shape[1]
    n_idx_heads = w_in.shape[2] - 3 * sb_w - 3 * dsa_w - IDX_DIM - 2 * d_model
    n_idx_heads = n_idx_heads // (IDX_DIM + 1)
    past = page_table.shape[1] * PAGE_SIZE
    l = 0

    wi = w_in[l]
    c0 = 3 * sb_w
    c1 = c0 + 3 * dsa_w
    c2 = c1 + n_idx_heads * IDX_DIM
    c3 = c2 + IDX_DIM
    c4 = c3 + n_idx_heads
    w_kiwi = jnp.concatenate(
        [wi[:, c2:c4], jnp.zeros((d_model, 2 * IDX_DIM - (c4 - c2)), wi.dtype)], axis=1).astype(BF16)
    wts = (wi[:, :c0].astype(BF16), wi[:, c0:c1].astype(BF16), wi[:, c1:c2].astype(BF16), w_kiwi,
           wi[:, c4:].astype(BF16))
    w_psb, w_pd, w_o = w_proj_sb[l].astype(BF16), w_proj_dsa[l].astype(BF16), w_out[l].astype(BF16)
    w_u, w_dn = w_up[l].astype(BF16), w_down[l].astype(BF16)
    g_mix, g_ffn, g_fin = norm_mix[l][None, :], norm_ffn[l][None, :], norm_final[None, :]
    cw, cb = conv_w[l], conv_b[l][None, :]

    xp = x_prompt[0]
    pp = _project_group(xp, g_mix, wts, _rope_tables(jnp.arange(s_p), HEAD_DIM), n_idx_heads, BF16)
    o_sb_p = _sb_prompt(pp["q_sb"], pp["k_sb_b"], pp["v_sb_b"], sb_heads)
    o_d_p = _dsa_prompt(pp["q_i"], pp["k_i_b"], pp["w_i"], pp["q_d"], pp["k_d_b"], pp["v_d_b"],
                        dsa_heads, n_idx_heads, min(TOPK_MAX, s_p // 4))
    x1_p, h2_p = _post_attention(xp, o_sb_p, o_d_p, pp["gates"], w_psb, w_pd, w_o, g_ffn)
    a_p, cg_p, cv_p = _ffn_up_prompt(h2_p, w_u, cw, cb, jnp.zeros((2, 2 * d_ff), F32))
    (y_p,) = _matmul_res(a_p, w_dn, x1_p, g_fin, emit_sum=False, norm_dtype=F32, tk=d_ff // 4)

    xs = x_sample[:, 0]
    pos_s = jnp.full((b_s,), past, I32)
    ps = _project_group(xs, g_mix, wts, _rope_tables(pos_s, HEAD_DIM), n_idx_heads, F32)
    r3 = lambda z: z[:, None, :]
    ck_sb = cache_k_sb[l].reshape(n_pool, PAGE_SIZE, sb_w)
    cv_sb = cache_v_sb[l].reshape(n_pool, PAGE_SIZE, sb_w)
    ck_d = cache_k_dsa[l].reshape(n_pool, PAGE_SIZE, dsa_w)
    cv_d = cache_v_dsa[l].reshape(n_pool, PAGE_SIZE, dsa_w)
    o_sb_s = _paged_attn("sb", r3(ps["q_sb"]), ck_sb, cv_sb, page_table)
    wcol = jnp.broadcast_to(ps["w_i"][:, :n_idx_heads, None], (b_s, n_idx_heads, LANES))
    sc_past, sc_new = _idx_scores_sample(ps["q_i"].reshape(b_s, n_idx_heads, IDX_DIM), wcol, r3(ps["k_i"]),
                                         cache_k_idx[l], page_table)
    scores = jnp.concatenate([sc_past[:, 0], sc_new[:, 0]], axis=1)
    bias = _topk_bias(scores, min(TOPK_MAX, (past + t_s) // 4))
    o_d_s = _paged_attn("dsa", r3(ps["q_d"]), ck_d, cv_d, page_table, bias=r3(bias[:, :past]),
                        bias_new=r3(bias[:, past:]), k_new=r3(ps["k_d"]), v_new=r3(ps["v_d"]))
    x1_s, h2_s = _post_attention(xs, o_sb_s[:, 0].astype(BF16), o_d_s[:, 0].astype(BF16), ps["gates"],
                                 w_psb, w_pd, w_o, g_ffn)
    st = state_conv[l].reshape(b_s, 2 * 2 * d_ff)
    a_s, ug_s, uv_s = _ffn_up_sample(h2_s, w_u, cw, cb, st)
    (y_s,) = _matmul_res(a_s, w_dn, x1_s, g_fin, emit_sum=False, norm_dtype=F32, tk=d_ff // 4)

    hd4 = lambda z, n: z.reshape(1, b_p, s_p, n, HEAD_DIM)
    hs4 = lambda z, n: z.reshape(1, b_s, t_s, n, HEAD_DIM)
    conv_p = jnp.concatenate([cg_p, cv_p], axis=1)[None, None]
    conv_s = jnp.stack([state_conv[l][:, 1], jnp.concatenate([ug_s, uv_s], axis=1)], axis=1)[None]
    return (y_p[None], y_s[:, None, :],
            hd4(pp["k_sb"], sb_heads), hd4(pp["v_sb"], sb_heads), hd4(pp["k_d"], dsa_heads),
            hd4(pp["v_d"], dsa_heads), pp["k_i"].reshape(1, b_p, s_p, IDX_DIM), conv_p,
            hs4(ps["k_sb"], sb_heads), hs4(ps["v_sb"], sb_heads), hs4(ps["k_d"], dsa_heads),
            hs4(ps["v_d"], dsa_heads), ps["k_i"].reshape(1, b_s, t_s, IDX_DIM), conv_s)
```

```python
import functools

import jax
import jax.numpy as jnp
from jax import lax
from jax.experimental import pallas as pl
from jax.experimental.pallas import tpu as pltpu

F32 = jnp.float32
BF16 = jnp.bfloat16
I32 = jnp.int32

HEAD_DIM = 128
IDX_DIM = 128
PAGE_SIZE = 128
TOPK_MAX = 256
ROPE_THETA = 10000.0
NORM_EPS = 1e-6
LANES = 128
HEAD_ROWS = 16
NEG_BIG = -1e30
INT_MIN = -2 ** 31
KEY_POS_INF = 0x7F800000
KEY_NEG_INF = INT_MIN + 0x00800000
EXP_ZERO_BELOW = -104.0
VALUE_BISECT_STEPS = 24
SELECT_ROWS = 128
VMEM_LIMIT = 56 * 1024 * 1024


def _cparams(sem):
    return pltpu.CompilerParams(dimension_semantics=sem, vmem_limit_bytes=VMEM_LIMIT)


def _rope_apply(x, cos2, sin2):
    return x * cos2 + pltpu.roll(x, HEAD_DIM // 2, axis=1) * sin2


def _softplus(z):
    return jnp.maximum(z, 0.0) + jnp.log(1.0 + jnp.exp(-jnp.abs(z)))


def _split_dot(a, m):
    hi = a.astype(BF16)
    lo = (a - hi.astype(F32)).astype(BF16)
    return (jnp.dot(hi, m, preferred_element_type=F32) + jnp.dot(lo, m, preferred_element_type=F32))


def _proj_body(*refs, segs, n_out, has_rope, wi_scale):
    x_ref, g_ref, w_ref = refs[:3]
    pos = 3
    if has_rope:
        cos_ref, sin_ref = refs[3:5]
        pos = 5
    out_refs = refs[pos:pos + n_out]
    h_ref = refs[pos + n_out]
    j = pl.program_id(1)

    @pl.when(j == 0)
    def _norm():
        x = x_ref[...]
        ms = jnp.mean(x * x, axis=-1, keepdims=True)
        h_ref[...] = ((x * lax.rsqrt(ms + NORM_EPS)) * g_ref[...]).astype(BF16)

    acc = jnp.dot(h_ref[...], w_ref[...], preferred_element_type=F32)
    tn = acc.shape[1]
    for s, (kind, oidx) in enumerate(segs):
        outs = [out_refs[o] for o in oidx]

        @pl.when(j == s)
        def _epilogue(kind=kind, outs=outs):
            if kind == "plain":
                for o in outs:
                    o[...] = acc.astype(o.dtype)
            elif kind == "sigmoid":
                y = jax.nn.sigmoid(acc)
                for o in outs:
                    o[...] = y.astype(o.dtype)
            elif kind == "rope":
                cos2, sin2 = cos_ref[...], sin_ref[...]
                for c in range(tn // HEAD_DIM):
                    sl = slice(c * HEAD_DIM, (c + 1) * HEAD_DIM)
                    y = _rope_apply(acc[:, sl], cos2, sin2)
                    for o in outs:
                        o[:, sl] = y.astype(o.dtype)
            elif kind == "kiwi":
                y = _rope_apply(acc[:, :IDX_DIM], cos_ref[...], sin_ref[...])
                outs[0][...] = y
                outs[1][...] = y.astype(BF16)
                outs[2][...] = acc[:, IDX_DIM:] * wi_scale
            else:
                raise ValueError(kind)


def _norm_proj(x, gain, w, tn, segs, outs, rope=None, wi_scale=1.0, tm=512):
    m, k = x.shape
    tm = min(tm, m)
    n_seg = len(segs)
    assert w.shape == (k, n_seg * tn) and m % tm == 0
    in_specs = [pl.BlockSpec((tm, k), lambda i, j: (i, 0)),
                pl.BlockSpec((1, k), lambda i, j: (0, 0)),
                pl.BlockSpec((k, tn), lambda i, j: (0, j))]
    args = [x, gain, w]
    if rope is not None:
        in_specs += [pl.BlockSpec((tm, HEAD_DIM), lambda i, j: (i, 0))] * 2
        args += list(rope)
    out_specs, out_shapes = [], []
    for dt, start, cnt in outs:
        width = tn if segs[start][0] != "kiwi" else IDX_DIM
        out_specs.append(pl.BlockSpec(
            (tm, width), lambda i, j, start=start, cnt=cnt: (i, jnp.clip(j - start, 0, cnt - 1))))
        out_shapes.append(jax.ShapeDtypeStruct((m, width * cnt), dt))
    body = functools.partial(_proj_body, segs=segs, n_out=len(outs), has_rope=rope is not None, wi_scale=wi_scale)
    return pl.pallas_call(
        body, grid=(m // tm, n_seg), in_specs=in_specs, out_specs=out_specs, out_shape=out_shapes,
        scratch_shapes=[pltpu.VMEM((tm, k), BF16)], name="norm_proj_" + segs[-1][0],
        compiler_params=_cparams(("arbitrary", "arbitrary")))(*args)


def _sb_prompt_body(q_ref, k_ref, v_ref, tri_ref, o_ref, *, tq, scale):
    i = pl.program_id(1)
    q = q_ref[...]
    tri = tri_ref[...]

    def block(kb, acc, carry, masked):
        start = pl.multiple_of(kb * tq, tq)
        k = k_ref[pl.ds(start, tq), :]
        v = v_ref[pl.ds(start, tq), :]
        z = lax.dot_general(q, k, (((1,), (1,)), ((), ())), preferred_element_type=F32) * scale
        lg = -_softplus(z)
        if masked:
            row = lax.broadcasted_iota(I32, (tq, tq), 0)
            col = lax.broadcasted_iota(I32, (tq, tq), 1)
            keep = col < row
            lg = jnp.where(keep, lg, 0.0)
        suffix = _split_dot(lg, tri)
        w = jnp.exp(z + lg + suffix + carry)
        if masked:
            w = jnp.where(keep, w, 0.0)
        acc = acc + jnp.dot(w.astype(BF16), v, preferred_element_type=F32)
        carry = carry + jnp.sum(lg, axis=1, keepdims=True)
        return acc, carry

    acc, carry = block(i, jnp.zeros((tq, HEAD_DIM), F32), jnp.zeros((tq, 1), F32), True)

    def more(c):
        return jnp.logical_and(c[0] >= 0, jnp.max(c[2]) > EXP_ZERO_BELOW)

    def step(c):
        acc, carry = block(c[0], c[1], c[2], False)
        return c[0] - 1, acc, carry

    _, acc, carry = lax.while_loop(more, step, (i - 1, acc, carry))
    o_ref[...] = acc.astype(o_ref.dtype)


def _sb_prompt(q, k, v, n_heads, tq=256):
    s = q.shape[0]
    tq = min(tq, s)
    tri = (lax.broadcasted_iota(I32, (tq, tq), 0) > lax.broadcasted_iota(I32, (tq, tq), 1)).astype(BF16)
    body = functools.partial(_sb_prompt_body, tq=tq, scale=HEAD_DIM ** -0.5)
    return pl.pallas_call(
        body, grid=(n_heads, s // tq), name="sb_prompt",
        in_specs=[pl.BlockSpec((tq, HEAD_DIM), lambda h, i: (i, h)),
                  pl.BlockSpec((s, HEAD_DIM), lambda h, i: (0, h)),
                  pl.BlockSpec((s, HEAD_DIM), lambda h, i: (0, h)),
                  pl.BlockSpec((tq, tq), lambda h, i: (0, 0))],
        out_specs=pl.BlockSpec((tq, HEAD_DIM), lambda h, i: (i, h)),
        out_shape=jax.ShapeDtypeStruct((s, n_heads * HEAD_DIM), BF16),
        compiler_params=_cparams(("arbitrary", "arbitrary")))(q, k, v, tri)


def _sort_key(score):
    b = lax.bitcast_convert_type(score, I32)
    return jnp.where(b < 0, (b ^ jnp.int32(0x7FFFFFFF)) + 1, b)


def _key_value(key):
    return lax.bitcast_convert_type(jnp.where(key < 0, (key - 1) ^ jnp.int32(0x7FFFFFFF), key), F32)


def _topk_threshold(count_ge, key_min, key_max, n_keys, topk):
    k = float(topk)

    def undecided(lo, hi, cnt):
        return jnp.logical_and(cnt > k, hi > lo + 1)

    def more(c):
        it, lo, hi, cnt = c
        return jnp.max(undecided(lo, hi, cnt).astype(F32)) > 0.0

    def step(c):
        it, lo, hi, cnt = c
        mid_v = jnp.clip(_sort_key(0.5 * _key_value(lo) + 0.5 * _key_value(hi)), lo + 1, hi - 1)
        mid_i = lo + lax.shift_right_logical(hi - lo, 1)
        mid = jnp.where(it < VALUE_BISECT_STEPS, mid_v, mid_i)
        mid = jnp.where(undecided(lo, hi, cnt), mid, lo)
        c_mid = count_ge(mid)
        ge = c_mid >= k
        return it + 1, jnp.where(ge, mid, lo), jnp.where(ge, hi, mid), jnp.where(ge, c_mid, cnt)

    _, lo, _, _ = lax.while_loop(more, step, (jnp.int32(0), key_min, key_max + 1, n_keys))
    return lo


def _dsa_prompt_body(qi_ref, ki_ref, wi_ref, q_ref, k_ref, v_ref, o_ref, key_ref, bias_ref, wb_ref,
                     *, tq, tk, n_idx_heads, topk, scale):
    i = pl.program_id(0)
    h = pl.program_id(1)
    n_chunks = (i * tq + tq - 1) // tk + 1
    qpos = i * tq + lax.broadcasted_iota(I32, (tq, tk), 0)

    @pl.when(h == 0)
    def _select():
        for hh in range(n_idx_heads):
            wb_ref[hh] = jnp.broadcast_to(wi_ref[:, hh:hh + 1], (tq, LANES))

        def score_chunk(c, _):
            start = pl.multiple_of(c * tk, tk)
            ki = ki_ref[pl.ds(start, tk), :]
            acc = jnp.zeros((tq, tk), F32)
            for hh in range(n_idx_heads):
                s = lax.dot_general(qi_ref[:, hh * IDX_DIM:(hh + 1) * IDX_DIM], ki, (((1,), (1,)), ((), ())),
                                    preferred_element_type=F32)
                wb = wb_ref[hh]
                acc = acc + jnp.maximum(s, 0.0) * jnp.concatenate([wb] * (tk // LANES), axis=1)
            kpos = start + lax.broadcasted_iota(I32, (tq, tk), 1)
            key_ref[c] = jnp.where(kpos <= qpos, _sort_key(acc), INT_MIN)
            return 0

        lax.fori_loop(0, n_chunks, score_chunk, 0)

        rb = min(SELECT_ROWS, tq)

        def row_pass(fn, init, finish):
            outs = []
            for r0 in range(0, tq, rb):
                def chunk(c, part, r0=r0):
                    keys = key_ref[c, r0:r0 + rb, :]
                    for u in range(tk // LANES):
                        part = fn(part, keys[:, u * LANES:(u + 1) * LANES], r0)
                    return part

                outs.append(finish(lax.fori_loop(0, n_chunks, chunk, jnp.full((rb, LANES), init, I32))))
            return jnp.concatenate(outs, axis=0)

        def count_ge(cand):
            cb = {r0: jnp.broadcast_to(cand[r0:r0 + rb], (rb, LANES)) for r0 in range(0, tq, rb)}
            return row_pass(lambda part, keys, r0: part + (keys >= cb[r0]).astype(I32), 0,
                            lambda part: jnp.sum(part.astype(F32), axis=1, keepdims=True))

        key_max = row_pass(lambda part, keys, r0: jnp.maximum(part, keys), KEY_NEG_INF,
                           lambda part: _sort_key(jnp.max(_key_value(part), axis=1, keepdims=True)))
        key_min = row_pass(lambda part, keys, r0: jnp.minimum(part, jnp.where(keys == INT_MIN, KEY_POS_INF, keys)),
                           KEY_POS_INF, lambda part: _sort_key(jnp.min(_key_value(part), axis=1, keepdims=True)))
        n_keys = (i * tq + 1 + lax.broadcasted_iota(I32, (tq, 1), 0)).astype(F32)
        thr = _topk_threshold(count_ge, key_min, key_max, n_keys, topk)

        def bias_chunk(c, _):
            bias_ref[c] = jnp.where(key_ref[c] >= thr, 0.0, NEG_BIG)
            return 0

        lax.fori_loop(0, n_chunks, bias_chunk, 0)

    q = q_ref[...]

    def attn_chunk(c, carry):
        m, l, acc = carry
        start = pl.multiple_of(c * tk, tk)
        k = k_ref[pl.ds(start, tk), :]
        v = v_ref[pl.ds(start, tk), :]
        s = lax.dot_general(q, k, (((1,), (1,)), ((), ())), preferred_element_type=F32) * scale
        s = s + bias_ref[c]
        m_new = jnp.maximum(m, jnp.max(s, axis=1, keepdims=True))
        alpha = jnp.exp(m - m_new)
        p = jnp.exp(s - m_new)
        l = alpha * l + jnp.sum(p, axis=1, keepdims=True)
        acc = alpha * acc + jnp.dot(p.astype(BF16), v, preferred_element_type=F32)
        return m_new, l, acc

    m, l, acc = lax.fori_loop(
        0, n_chunks, attn_chunk,
        (jnp.full((tq, 1), -jnp.inf, F32), jnp.zeros((tq, 1), F32), jnp.zeros((tq, HEAD_DIM), F32)))
    o_ref[...] = (acc / l).astype(o_ref.dtype)


def _dsa_prompt(qi, ki, wi, q, k, v, n_heads, n_idx_heads, topk, tq=256, tk=512):
    s = q.shape[0]
    tq = min(tq, s)
    tk = min(tk, s)
    body = functools.partial(_dsa_prompt_body, tq=tq, tk=tk, n_idx_heads=n_idx_heads, topk=topk,
                             scale=HEAD_DIM ** -0.5)
    return pl.pallas_call(
        body, grid=(s // tq, n_heads), name="dsa_prompt",
        in_specs=[pl.BlockSpec((tq, n_idx_heads * IDX_DIM), lambda i, h: (i, 0)),
                  pl.BlockSpec((s, IDX_DIM), lambda i, h: (0, 0)),
                  pl.BlockSpec((tq, LANES), lambda i, h: (i, 0)),
                  pl.BlockSpec((tq, HEAD_DIM), lambda i, h: (i, h)),
                  pl.BlockSpec((s, HEAD_DIM), lambda i, h: (0, h)),
                  pl.BlockSpec((s, HEAD_DIM), lambda i, h: (0, h))],
        out_specs=pl.BlockSpec((tq, HEAD_DIM), lambda i, h: (i, h)),
        out_shape=jax.ShapeDtypeStruct((s, n_heads * HEAD_DIM), BF16),
        scratch_shapes=[pltpu.VMEM((s // tk, tq, tk), I32), pltpu.VMEM((s // tk, tq, tk), F32),
                        pltpu.VMEM((n_idx_heads, tq, LANES), F32)],
        compiler_params=_cparams(("arbitrary", "arbitrary")))(qi, ki, wi, q, k, v)


def _head_block_mask(width):
    row = lax.broadcasted_iota(I32, (HEAD_ROWS, width), 0)
    col = lax.broadcasted_iota(I32, (HEAD_ROWS, width), 1)
    return row == col // HEAD_DIM


def _head_rows_query(q_ref, blk):
    return jnp.where(blk, jnp.broadcast_to(q_ref[...], blk.shape), 0.0)


def _paged_dsa_body(pt_ref, q_ref, bias_ref, bnew_ref, knew_ref, vnew_ref, *refs, pp, scale):
    del pt_ref
    k_refs, v_refs = refs[:pp], refs[pp:2 * pp]
    o_ref, acc_ref, m_ref, l_ref = refs[2 * pp:]
    p = pl.program_id(1)
    blk = _head_block_mask(q_ref.shape[-1])
    qf = _head_rows_query(q_ref, blk)
    qb = qf.astype(BF16)

    @pl.when(p == 0)
    def _init():
        acc_ref[...] = jnp.zeros_like(acc_ref)
        l_ref[...] = jnp.zeros_like(l_ref)
        m_ref[...] = jnp.full_like(m_ref, -jnp.inf)

    s = jnp.concatenate(
        [lax.dot_general(qb, k_refs[r][...].astype(BF16), (((1,), (1,)), ((), ())), preferred_element_type=F32)
         for r in range(pp)], axis=1) * scale + bias_ref[...]
    m_old = m_ref[...]
    m_new = jnp.maximum(m_old, jnp.max(s, axis=1, keepdims=True))
    alpha = jnp.exp(m_old - m_new)
    w = jnp.exp(s - m_new)
    l_ref[...] = alpha * l_ref[...] + jnp.sum(w, axis=1, keepdims=True)
    w = w.astype(BF16)
    pv = jnp.dot(w[:, :PAGE_SIZE], v_refs[0][...].astype(BF16), preferred_element_type=F32)
    for r in range(1, pp):
        pv += jnp.dot(w[:, r * PAGE_SIZE:(r + 1) * PAGE_SIZE], v_refs[r][...].astype(BF16),
                      preferred_element_type=F32)
    acc_ref[...] = alpha * acc_ref[...] + pv
    m_ref[...] = m_new

    @pl.when(p == pl.num_programs(1) - 1)
    def _finish():
        s_new = jnp.sum(qf * knew_ref[...], axis=1, keepdims=True) * scale + bnew_ref[:, 0:1]
        m_fin = jnp.maximum(m_new, s_new)
        a_fin = jnp.exp(m_new - m_fin)
        w_new = jnp.exp(s_new - m_fin)
        l = a_fin * l_ref[...] + w_new
        acc = (a_fin * acc_ref[...] + w_new * vnew_ref[...]) / l
        o_ref[...] = jnp.sum(jnp.where(blk, acc, 0.0), axis=0, keepdims=True)


def _paged_dsa(q, cache_k, cache_v, page_table, bias, bias_new, k_new, v_new, pp=8):
    b, _, width = q.shape
    n_pages = page_table.shape[1]
    pp = min(pp, n_pages)
    assert n_pages % pp == 0
    row_spec = pl.BlockSpec((None, 1, width), lambda bi, p, pt: (bi, 0, 0))
    page_specs = [pl.BlockSpec((None, PAGE_SIZE, width), lambda bi, p, pt, r=r: (pt[bi, p * pp + r], 0, 0))
                  for r in range(pp)]
    grid_spec = pltpu.PrefetchScalarGridSpec(
        num_scalar_prefetch=1, grid=(b, n_pages // pp),
        in_specs=[row_spec, pl.BlockSpec((None, 1, pp * PAGE_SIZE), lambda bi, p, pt: (bi, 0, p)),
                  pl.BlockSpec((None, 1, LANES), lambda bi, p, pt: (bi, 0, 0)), row_spec, row_spec]
        + page_specs + page_specs,
        out_specs=row_spec,
        scratch_shapes=[pltpu.VMEM((HEAD_ROWS, width), F32), pltpu.VMEM((HEAD_ROWS, 1), F32),
                        pltpu.VMEM((HEAD_ROWS, 1), F32)])
    body = functools.partial(_paged_dsa_body, pp=pp, scale=HEAD_DIM ** -0.5)
    return pl.pallas_call(
        body, grid_spec=grid_spec, out_shape=jax.ShapeDtypeStruct((b, 1, width), F32), name="paged_dsa",
        compiler_params=_cparams(("arbitrary", "arbitrary")))(
            page_table, q, bias, bias_new, k_new, v_new, *([cache_k] * pp), *([cache_v] * pp))


def _paged_sb_body(pt_ref, q_ref, tri_ref, ck_hbm, cv_hbm, o_ref, kbuf, vbuf, sem, *, n_pages, scale):
    b = pl.program_id(0)
    width = q_ref.shape[-1]
    blk = _head_block_mask(width)
    qb = _head_rows_query(q_ref, blk).astype(BF16)
    tri = tri_ref[...]

    def page_copies(page, slot):
        phys = pt_ref[b, page]
        return (pltpu.make_async_copy(ck_hbm.at[phys], kbuf.at[slot], sem.at[0, slot]),
                pltpu.make_async_copy(cv_hbm.at[phys], vbuf.at[slot], sem.at[1, slot]))

    def slot_of(page):
        return lax.rem(n_pages - 1 - page, 2)

    for cp in page_copies(n_pages - 1, 0):
        cp.start()

    def more(c):
        return jnp.logical_and(c[0] >= 0, jnp.max(c[2]) > EXP_ZERO_BELOW)

    def step(c):
        page, acc, carry = c
        slot = slot_of(page)
        for cp in page_copies(page, slot):
            cp.wait()

        @pl.when(page > 0)
        def _prefetch():
            for cp in page_copies(page - 1, 1 - slot):
                cp.start()

        k = kbuf[slot].astype(BF16)
        v = vbuf[slot].astype(BF16)
        z = lax.dot_general(qb, k, (((1,), (1,)), ((), ())), preferred_element_type=F32) * scale
        lg = -_softplus(z)
        w = jnp.exp(z + lg + _split_dot(lg, tri) + carry)
        acc = acc + jnp.dot(w.astype(BF16), v, preferred_element_type=F32)
        return page - 1, acc, carry + jnp.sum(lg, axis=1, keepdims=True)

    page, acc, _ = lax.while_loop(
        more, step, (jnp.int32(n_pages - 1), jnp.zeros((HEAD_ROWS, width), F32), jnp.zeros((HEAD_ROWS, 1), F32)))

    @pl.when(page >= 0)
    def _drain():
        for cp in page_copies(page, slot_of(page)):
            cp.wait()

    o_ref[...] = jnp.sum(jnp.where(blk, acc, 0.0), axis=0, keepdims=True)


def _paged_sb(q, cache_k, cache_v, page_table):
    b, _, width = q.shape
    n_pages = page_table.shape[1]
    tri = (lax.broadcasted_iota(I32, (PAGE_SIZE, PAGE_SIZE), 0)
           > lax.broadcasted_iota(I32, (PAGE_SIZE, PAGE_SIZE), 1)).astype(BF16)
    row_spec = pl.BlockSpec((None, 1, width), lambda bi, pt: (bi, 0, 0))
    grid_spec = pltpu.PrefetchScalarGridSpec(
        num_scalar_prefetch=1, grid=(b,),
        in_specs=[row_spec, pl.BlockSpec((PAGE_SIZE, PAGE_SIZE), lambda bi, pt: (0, 0)),
                  pl.BlockSpec(memory_space=pl.ANY), pl.BlockSpec(memory_space=pl.ANY)],
        out_specs=row_spec,
        scratch_shapes=[pltpu.VMEM((2, PAGE_SIZE, width), F32), pltpu.VMEM((2, PAGE_SIZE, width), F32),
                        pltpu.SemaphoreType.DMA((2, 2))])
    body = functools.partial(_paged_sb_body, n_pages=n_pages, scale=HEAD_DIM ** -0.5)
    return pl.pallas_call(
        body, grid_spec=grid_spec, out_shape=jax.ShapeDtypeStruct((b, 1, width), F32), name="paged_sb",
        compiler_params=_cparams(("arbitrary",)))(page_table, q, tri, cache_k, cache_v)


def _idx_scores_body(pt_ref, qi_ref, wcol_ref, knew_ref, *refs, pp):
    del pt_ref
    page_refs = refs[:pp]
    o_ref, onew_ref = refs[pp:pp + 2]
    qi = qi_ref[...]
    qb = qi.astype(BF16)
    wcol = wcol_ref[...]
    for r in range(pp):
        kp = page_refs[r][...].astype(BF16)
        s = lax.dot_general(qb, kp, (((1,), (1,)), ((), ())), preferred_element_type=F32)
        o_ref[:, r * PAGE_SIZE:(r + 1) * PAGE_SIZE] = jnp.sum(jnp.maximum(s, 0.0) * wcol, axis=0, keepdims=True)

    @pl.when(pl.program_id(1) == 0)
    def _new_key():
        s = jnp.sum(qi * knew_ref[...], axis=1, keepdims=True)
        sc = jnp.sum(jnp.maximum(s, 0.0) * wcol[:, 0:1], axis=0, keepdims=True)
        lane = lax.broadcasted_iota(I32, (1, LANES), 1)
        onew_ref[...] = jnp.where(lane == 0, sc, -jnp.inf)


def _idx_scores_sample(qi, wcol, ki_new, cache_ki, page_table, pp=8):
    b, hi, e = qi.shape
    n_pages = page_table.shape[1]
    pp = min(pp, n_pages)
    steps = n_pages // pp

    def page_map(r):
        return lambda bi, p, pt: (pt[bi, p * pp + r], 0, 0)

    grid_spec = pltpu.PrefetchScalarGridSpec(
        num_scalar_prefetch=1, grid=(b, steps),
        in_specs=[pl.BlockSpec((None, hi, e), lambda bi, p, pt: (bi, 0, 0)),
                  pl.BlockSpec((None, hi, LANES), lambda bi, p, pt: (bi, 0, 0)),
                  pl.BlockSpec((None, 1, e), lambda bi, p, pt: (bi, 0, 0))]
        + [pl.BlockSpec((None, PAGE_SIZE, e), page_map(r)) for r in range(pp)],
        out_specs=[pl.BlockSpec((None, 1, pp * PAGE_SIZE), lambda bi, p, pt: (bi, 0, p)),
                   pl.BlockSpec((None, 1, LANES), lambda bi, p, pt: (bi, 0, 0))])
    return pl.pallas_call(
        functools.partial(_idx_scores_body, pp=pp), grid_spec=grid_spec, name="idx_scores_sample",
        out_shape=[jax.ShapeDtypeStruct((b, 1, n_pages * PAGE_SIZE), F32), jax.ShapeDtypeStruct((b, 1, LANES), F32)],
        compiler_params=_cparams(("arbitrary", "arbitrary")))(page_table, qi, wcol, ki_new, *([cache_ki] * pp))


def _topk_bias_body(s_ref, o_ref, *, topk):
    keys = _sort_key(s_ref[...])
    keys = jnp.where(s_ref[...] == -jnp.inf, INT_MIN, keys)

    def count_ge(cand):
        return jnp.sum((keys >= cand).astype(F32), axis=1, keepdims=True)

    real = s_ref[...] != -jnp.inf
    key_max = _sort_key(jnp.max(s_ref[...], axis=1, keepdims=True))
    key_min = _sort_key(jnp.min(jnp.where(real, s_ref[...], jnp.inf), axis=1, keepdims=True))
    n_keys = jnp.sum(real.astype(F32), axis=1, keepdims=True)
    thr = _topk_threshold(count_ge, key_min, key_max, n_keys, topk)
    o_ref[...] = jnp.where(keys >= thr, 0.0, NEG_BIG)


def _topk_bias(scores, topk):
    return pl.pallas_call(
        functools.partial(_topk_bias_body, topk=topk), name="topk_bias",
        out_shape=jax.ShapeDtypeStruct(scores.shape, F32))(scores)


def _merge_gate_body(osb_ref, od_ref, wsb_ref, wd_ref, gsb_ref, gd_ref, o_ref):
    y_sb = jnp.dot(osb_ref[...], wsb_ref[...], preferred_element_type=F32)
    y_d = jnp.dot(od_ref[...], wd_ref[...], preferred_element_type=F32)
    o_ref[...] = (gsb_ref[...] * y_sb + gd_ref[...] * y_d).astype(o_ref.dtype)


def _merge_gate(o_sb, o_d, w_sb, w_d, gates, tm=512, tn=1024):
    m, ksb = o_sb.shape
    kd = o_d.shape[1]
    d = w_sb.shape[1]
    tm = min(tm, m)
    tn = min(tn, d)
    nj = d // tn
    return pl.pallas_call(
        _merge_gate_body, grid=(m // tm, nj), name="merge_gate",
        in_specs=[pl.BlockSpec((tm, ksb), lambda i, j: (i, 0)),
                  pl.BlockSpec((tm, kd), lambda i, j: (i, 0)),
                  pl.BlockSpec((ksb, tn), lambda i, j: (0, j)),
                  pl.BlockSpec((kd, tn), lambda i, j: (0, j)),
                  pl.BlockSpec((tm, tn), lambda i, j: (i, j)),
                  pl.BlockSpec((tm, tn), lambda i, j: (i, nj + j))],
        out_specs=pl.BlockSpec((tm, tn), lambda i, j: (i, j)),
        out_shape=jax.ShapeDtypeStruct((m, d), BF16),
        compiler_params=_cparams(("arbitrary", "arbitrary")))(o_sb, o_d, w_sb, w_d, gates, gates)


def _matmul_res_body(a_ref, w_ref, r_ref, g_ref, *refs, emit_sum, emit_norm):
    outs = refs[:-1]
    acc_ref = refs[-1]
    kk = pl.program_id(1)

    @pl.when(kk == 0)
    def _init():
        acc_ref[...] = jnp.zeros_like(acc_ref)

    acc_ref[...] += jnp.dot(a_ref[...], w_ref[...], preferred_element_type=F32)

    @pl.when(kk == pl.num_programs(1) - 1)
    def _finish():
        y = r_ref[...] + acc_ref[...]
        o = 0
        if emit_sum:
            outs[o][...] = y
            o += 1
        if emit_norm:
            ms = jnp.mean(y * y, axis=-1, keepdims=True)
            outs[o][...] = ((y * lax.rsqrt(ms + NORM_EPS)) * g_ref[...]).astype(outs[o].dtype)


def _matmul_res(a, w, res, gain, *, emit_sum, norm_dtype, tm=512, tk=None):
    m, k = a.shape
    n = w.shape[1]
    tm = min(tm, m)
    tk = k if tk is None else tk
    assert k % tk == 0
    out_shapes, out_specs = [], []
    if emit_sum:
        out_shapes.append(jax.ShapeDtypeStruct((m, n), F32))
    if norm_dtype is not None:
        out_shapes.append(jax.ShapeDtypeStruct((m, n), norm_dtype))
    out_specs = [pl.BlockSpec((tm, n), lambda i, kk: (i, 0)) for _ in out_shapes]
    body = functools.partial(_matmul_res_body, emit_sum=emit_sum, emit_norm=norm_dtype is not None)
    return pl.pallas_call(
        body, grid=(m // tm, k // tk), name="matmul_res",
        in_specs=[pl.BlockSpec((tm, tk), lambda i, kk: (i, kk)),
                  pl.BlockSpec((tk, n), lambda i, kk: (kk, 0)),
                  pl.BlockSpec((tm, n), lambda i, kk: (i, 0)),
                  pl.BlockSpec((1, n), lambda i, kk: (0, 0))],
        out_specs=out_specs, out_shape=out_shapes,
        scratch_shapes=[pltpu.VMEM((tm, n), F32)],
        compiler_params=_cparams(("arbitrary", "arbitrary")))(a, w, res, gain)


def _silu_gate(cg, cv):
    return (cg * jax.nn.sigmoid(cg)) * cv


def _ffn_up_prompt_body(h_ref, wg_ref, wv_ref, cwg_ref, cwv_ref, cbg_ref, cbv_ref, pg_ref, pv_ref,
                        a_ref, og_ref, ov_ref, carry_g, carry_v, *, tm):
    mi = pl.program_id(1)
    h = h_ref[...]
    row = lax.broadcasted_iota(I32, (tm, 1), 0)

    @pl.when(mi == 0)
    def _init():
        carry_g[0:2, :] = pg_ref[...]
        carry_v[0:2, :] = pv_ref[...]

    def conv(w_ref, cw_ref, cb_ref, carry):
        u = jnp.dot(h, w_ref[...], preferred_element_type=F32)
        c_prev, c_last = carry[0:1, :], carry[1:2, :]
        u1 = jnp.where(row == 0, c_last, pltpu.roll(u, 1, axis=0))
        u2 = jnp.where(row == 0, c_prev, jnp.where(row == 1, c_last, pltpu.roll(u, 2, axis=0)))
        carry[0:2, :] = u[tm - 2:tm, :]
        return cb_ref[...] + u2 * cw_ref[0:1, :] + u1 * cw_ref[1:2, :] + u * cw_ref[2:3, :]

    cg = conv(wg_ref, cwg_ref, cbg_ref, carry_g)
    cv = conv(wv_ref, cwv_ref, cbv_ref, carry_v)
    a_ref[...] = _silu_gate(cg, cv).astype(a_ref.dtype)

    @pl.when(mi == pl.num_programs(1) - 1)
    def _state():
        og_ref[...] = carry_g[0:2, :]
        ov_ref[...] = carry_v[0:2, :]


def _ffn_up_prompt(h, w_up, conv_w, conv_b, conv_prev, tm=512, tf=512):
    m, d = h.shape
    f = w_up.shape[1] // 2
    tm = min(tm, m)
    tf = min(tf, f)
    nf = f // tf
    assert f % tf == 0 and m % tm == 0
    lo = lambda fi, mi: (0, fi)
    hi = lambda fi, mi: (0, nf + fi)
    body = functools.partial(_ffn_up_prompt_body, tm=tm)
    return pl.pallas_call(
        body, grid=(nf, m // tm), name="ffn_up_prompt",
        in_specs=[pl.BlockSpec((tm, d), lambda fi, mi: (mi, 0)),
                  pl.BlockSpec((d, tf), lo), pl.BlockSpec((d, tf), hi),
                  pl.BlockSpec((3, tf), lo), pl.BlockSpec((3, tf), hi),
                  pl.BlockSpec((1, tf), lo), pl.BlockSpec((1, tf), hi),
                  pl.BlockSpec((2, tf), lo), pl.BlockSpec((2, tf), hi)],
        out_specs=[pl.BlockSpec((tm, tf), lambda fi, mi: (mi, fi)),
                   pl.BlockSpec((2, tf), lo), pl.BlockSpec((2, tf), lo)],
        out_shape=[jax.ShapeDtypeStruct((m, f), BF16), jax.ShapeDtypeStruct((2, f), F32),
                   jax.ShapeDtypeStruct((2, f), F32)],
        scratch_shapes=[pltpu.VMEM((8, tf), F32), pltpu.VMEM((8, tf), F32)],
        compiler_params=_cparams(("arbitrary", "arbitrary")))(
            h, w_up, w_up, conv_w, conv_w, conv_b, conv_b, conv_prev, conv_prev)


def _ffn_up_sample_body(h_ref, wg_ref, wv_ref, cwg_ref, cwv_ref, cbg_ref, cbv_ref,
                        s0g_ref, s0v_ref, s1g_ref, s1v_ref, a_ref, ug_ref, uv_ref):
    h = h_ref[...]

    def conv(w_ref, cw_ref, cb_ref, s0_ref, s1_ref, u_ref):
        u = jnp.dot(h, w_ref[...], preferred_element_type=F32)
        u_ref[...] = u
        return cb_ref[...] + s0_ref[...] * cw_ref[0:1, :] + s1_ref[...] * cw_ref[1:2, :] + u * cw_ref[2:3, :]

    cg = conv(wg_ref, cwg_ref, cbg_ref, s0g_ref, s1g_ref, ug_ref)
    cv = conv(wv_ref, cwv_ref, cbv_ref, s0v_ref, s1v_ref, uv_ref)
    a_ref[...] = _silu_gate(cg, cv).astype(a_ref.dtype)


def _ffn_up_sample(h, w_up, conv_w, conv_b, state, tf=512):
    b, d = h.shape
    f = w_up.shape[1] // 2
    tf = min(tf, f)
    nf = f // tf
    col = lambda off: (lambda fi: (0, off * nf + fi))
    wspec = lambda off: pl.BlockSpec((d, tf), col(off))
    sspec = lambda off: pl.BlockSpec((b, tf), col(off))
    return pl.pallas_call(
        _ffn_up_sample_body, grid=(nf,), name="ffn_up_sample",
        in_specs=[pl.BlockSpec((b, d), lambda fi: (0, 0)), wspec(0), wspec(1),
                  pl.BlockSpec((3, tf), col(0)), pl.BlockSpec((3, tf), col(1)),
                  pl.BlockSpec((1, tf), col(0)), pl.BlockSpec((1, tf), col(1)),
                  sspec(0), sspec(1), sspec(2), sspec(3)],
        out_specs=[sspec(0), sspec(0), sspec(0)],
        out_shape=[jax.ShapeDtypeStruct((b, f), BF16), jax.ShapeDtypeStruct((b, f), F32),
                   jax.ShapeDtypeStruct((b, f), F32)],
        compiler_params=_cparams(("arbitrary",)))(
            h, w_up, w_up, conv_w, conv_w, conv_b, conv_b, state, state, state, state)


def _rope_tables(pos, dim):
    inv = ROPE_THETA ** (-jnp.arange(0, dim, 2, dtype=F32) / dim)
    ang = pos.astype(F32)[:, None] * inv[None, :]
    cos, sin = jnp.cos(ang), jnp.sin(ang)
    return jnp.concatenate([cos, cos], axis=1), jnp.concatenate([-sin, sin], axis=1)


def _project_group(x, gain, wts, rope, n_idx_heads, q_dtype):
    w_sb, w_dsa, w_qi, w_kiwi, w_gate = wts
    tn = w_sb.shape[1] // 3
    q_sb, k_sb, k_sb_b, v_sb, v_sb_b = _norm_proj(
        x, gain, w_sb, tn,
        segs=[("plain", [0]), ("plain", [1, 2]), ("plain", [3, 4])],
        outs=[(q_dtype, 0, 1), (F32, 1, 1), (BF16, 1, 1), (F32, 2, 1), (BF16, 2, 1)])
    q_d, k_d, k_d_b, v_d, v_d_b = _norm_proj(
        x, gain, w_dsa, tn,
        segs=[("rope", [0]), ("rope", [1, 2]), ("plain", [3, 4])],
        outs=[(q_dtype, 0, 1), (F32, 1, 1), (BF16, 1, 1), (F32, 2, 1), (BF16, 2, 1)], rope=rope)
    n_qi = w_qi.shape[1] // tn
    (q_i,) = _norm_proj(x, gain, w_qi, tn, segs=[("rope", [0])] * n_qi, outs=[(q_dtype, 0, n_qi)], rope=rope)
    k_i, k_i_b, w_i = _norm_proj(
        x, gain, w_kiwi, 2 * IDX_DIM, segs=[("kiwi", [0, 1, 2])],
        outs=[(F32, 0, 1), (BF16, 0, 1), (F32, 0, 1)], rope=rope,
        wi_scale=n_idx_heads ** -0.5 * IDX_DIM ** -0.5)
    n_g = w_gate.shape[1] // tn
    (gates,) = _norm_proj(x, gain, w_gate, tn, segs=[("sigmoid", [0])] * n_g, outs=[(F32, 0, n_g)])
    return dict(q_sb=q_sb, k_sb=k_sb, k_sb_b=k_sb_b, v_sb=v_sb, v_sb_b=v_sb_b, q_d=q_d, k_d=k_d, k_d_b=k_d_b,
                v_d=v_d, v_d_b=v_d_b, q_i=q_i, k_i=k_i, k_i_b=k_i_b, w_i=w_i, gates=gates)


def _post_attention(x, o_sb, o_d, gates, w_proj_sb, w_proj_dsa, w_out, norm_ffn):
    merged = _merge_gate(o_sb, o_d, w_proj_sb, w_proj_dsa, gates)
    return _matmul_res(merged, w_out, x, norm_ffn, emit_sum=True, norm_dtype=BF16)


def kernel(x_prompt, x_sample, cache_k_sb, cache_v_sb, cache_k_dsa, cache_v_dsa, cache_k_idx, state_conv,
           page_table, norm_mix, w_in, w_proj_sb, w_proj_dsa, w_out, norm_ffn, w_up, conv_w, conv_b, w_down,
           norm_final):
    b_p, s_p, d_model = x_prompt.shape
    b_s, t_s, _ = x_sample.shape
    depth = w_in.shape[0]
    assert b_p == 1 and t_s == 1 and depth == 1
    n_pool, page, sb_heads, hd = cache_k_sb.shape[1:]
    dsa_heads = cache_k_dsa.shape[3]
    assert hd == HEAD_DIM and page == PAGE_SIZE and cache_k_idx.shape[-1] == IDX_DIM
    sb_w, dsa_w = sb_heads * HEAD_DIM, dsa_heads * HEAD_DIM
    d_ff = w_down.shape[1]
    n_idx_heads = w_in.shape[2] - 3 * sb_w - 3 * dsa_w - IDX_DIM - 2 * d_model
    n_idx_heads = n_idx_heads // (IDX_DIM + 1)
    past = page_table.shape[1] * PAGE_SIZE
    l = 0

    wi = w_in[l]
    c0 = 3 * sb_w
    c1 = c0 + 3 * dsa_w
    c2 = c1 + n_idx_heads * IDX_DIM
    c3 = c2 + IDX_DIM
    c4 = c3 + n_idx_heads
    w_kiwi = jnp.concatenate(
        [wi[:, c2:c4], jnp.zeros((d_model, 2 * IDX_DIM - (c4 - c2)), wi.dtype)], axis=1).astype(BF16)
    wts = (wi[:, :c0].astype(BF16), wi[:, c0:c1].astype(BF16), wi[:, c1:c2].astype(BF16), w_kiwi,
           wi[:, c4:].astype(BF16))
    w_psb, w_pd, w_o = w_proj_sb[l].astype(BF16), w_proj_dsa[l].astype(BF16), w_out[l].astype(BF16)
    w_u, w_dn = w_up[l].astype(BF16), w_down[l].astype(BF16)
    g_mix, g_ffn, g_fin = norm_mix[l][None, :], norm_ffn[l][None, :], norm_final[None, :]
    cw, cb = conv_w[l], conv_b[l][None, :]

    xp = x_prompt[0]
    pp = _project_group(xp, g_mix, wts, _rope_tables(jnp.arange(s_p), HEAD_DIM), n_idx_heads, BF16)
    o_sb_p = _sb_prompt(pp["q_sb"], pp["k_sb_b"], pp["v_sb_b"], sb_heads)
    o_d_p = _dsa_prompt(pp["q_i"], pp["k_i_b"], pp["w_i"], pp["q_d"], pp["k_d_b"], pp["v_d_b"],
                        dsa_heads, n_idx_heads, min(TOPK_MAX, s_p // 4))
    x1_p, h2_p = _post_attention(xp, o_sb_p, o_d_p, pp["gates"], w_psb, w_pd, w_o, g_ffn)
    a_p, cg_p, cv_p = _ffn_up_prompt(h2_p, w_u, cw, cb, jnp.zeros((2, 2 * d_ff), F32))
    (y_p,) = _matmul_res(a_p, w_dn, x1_p, g_fin, emit_sum=False, norm_dtype=F32, tk=d_ff // 4)

    xs = x_sample[:, 0]
    pos_s = jnp.full((b_s,), past, I32)
    ps = _project_group(xs, g_mix, wts, _rope_tables(pos_s, HEAD_DIM), n_idx_heads, F32)
    r3 = lambda z: z[:, None, :]
    ck_sb = cache_k_sb.reshape(depth * n_pool, PAGE_SIZE, sb_w)
    cv_sb = cache_v_sb.reshape(depth * n_pool, PAGE_SIZE, sb_w)
    ck_d = cache_k_dsa.reshape(depth * n_pool, PAGE_SIZE, dsa_w)
    cv_d = cache_v_dsa.reshape(depth * n_pool, PAGE_SIZE, dsa_w)
    ck_i = cache_k_idx.reshape(depth * n_pool, PAGE_SIZE, IDX_DIM)
    page_table = page_table + l * n_pool
    o_sb_s = _paged_sb(r3(ps["q_sb"]), ck_sb, cv_sb, page_table)
    wcol = jnp.broadcast_to(ps["w_i"][:, :n_idx_heads, None], (b_s, n_idx_heads, LANES))
    sc_past, sc_new = _idx_scores_sample(ps["q_i"].reshape(b_s, n_idx_heads, IDX_DIM), wcol, r3(ps["k_i"]),
                                         ck_i, page_table)
    scores = jnp.concatenate([sc_past[:, 0], sc_new[:, 0]], axis=1)
    bias = _topk_bias(scores, min(TOPK_MAX, (past + t_s) // 4))
    o_d_s = _paged_dsa(r3(ps["q_d"]), ck_d, cv_d, page_table, r3(bias[:, :past]), r3(bias[:, past:]),
                       r3(ps["k_d"]), r3(ps["v_d"]))
    x1_s, h2_s = _post_attention(xs, o_sb_s[:, 0].astype(BF16), o_d_s[:, 0].astype(BF16), ps["gates"],
                                 w_psb, w_pd, w_o, g_ffn)
    st = state_conv[l].reshape(b_s, 2 * 2 * d_ff)
    a_s, ug_s, uv_s = _ffn_up_sample(h2_s, w_u, cw, cb, st)
    (y_s,) = _matmul_res(a_s, w_dn, x1_s, g_fin, emit_sum=False, norm_dtype=F32, tk=d_ff // 4)

    hd4 = lambda z, n: z.reshape(1, b_p, s_p, n, HEAD_DIM)
    hs4 = lambda z, n: z.reshape(1, b_s, t_s, n, HEAD_DIM)
    conv_p = jnp.concatenate([cg_p, cv_p], axis=1)[None, None]
    conv_s = jnp.stack([state_conv[l][:, 1], jnp.concatenate([ug_s, uv_s], axis=1)], axis=1)[None]
    return (y_p[None], y_s[:, None, :],
            hd4(pp["k_sb"], sb_heads), hd4(pp["v_sb"], sb_heads), hd4(pp["k_d"], dsa_heads),
            hd4(pp["v_d"], dsa_heads), pp["k_i"].reshape(1, b_p, s_p, IDX_DIM), conv_p,
            hs4(ps["k_sb"], sb_heads), hs4(ps["v_sb"], sb_heads), hs4(ps["k_d"], dsa_heads),
            hs4(ps["v_d"], dsa_heads), ps["k_i"].reshape(1, b_s, t_s, IDX_DIM), conv_s)
```

```python
import functools

import jax
import jax.numpy as jnp
from jax import lax
from jax.experimental import pallas as pl
from jax.experimental.pallas import tpu as pltpu

F32 = jnp.float32
BF16 = jnp.bfloat16
I32 = jnp.int32

HEAD_DIM = 128
IDX_DIM = 128
PAGE_SIZE = 128
TOPK_MAX = 256
ROPE_THETA = 10000.0
NORM_EPS = 1e-6
LANES = 128
HEAD_ROWS = 16
NEG_BIG = -1e30
INT_MIN = -2 ** 31
KEY_POS_INF = 0x7F800000
KEY_NEG_INF = INT_MIN + 0x00800000
EXP_ZERO_BELOW = -104.0
VALUE_BISECT_STEPS = 24
SELECT_ROWS = 128
VMEM_LIMIT = 56 * 1024 * 1024


def _cparams(sem):
    return pltpu.CompilerParams(dimension_semantics=sem, vmem_limit_bytes=VMEM_LIMIT)


def _rope_apply(x, cos2, sin2):
    return x * cos2 + pltpu.roll(x, HEAD_DIM // 2, axis=1) * sin2


def _softplus(z):
    return jnp.maximum(z, 0.0) + jnp.log(1.0 + jnp.exp(-jnp.abs(z)))


def _split_dot(a, m):
    hi = a.astype(BF16)
    lo = (a - hi.astype(F32)).astype(BF16)
    return (jnp.dot(hi, m, preferred_element_type=F32) + jnp.dot(lo, m, preferred_element_type=F32))


def _proj_body(*refs, segs, n_out, has_rope, wi_scale):
    x_ref, g_ref, w_ref = refs[:3]
    pos = 3
    if has_rope:
        cos_ref, sin_ref = refs[3:5]
        pos = 5
    out_refs = refs[pos:pos + n_out]
    h_ref = refs[pos + n_out]
    j = pl.program_id(1)

    @pl.when(j == 0)
    def _norm():
        x = x_ref[...]
        ms = jnp.mean(x * x, axis=-1, keepdims=True)
        h_ref[...] = ((x * lax.rsqrt(ms + NORM_EPS)) * g_ref[...]).astype(BF16)

    acc = jnp.dot(h_ref[...], w_ref[...], preferred_element_type=F32)
    tn = acc.shape[1]
    for s, (kind, oidx) in enumerate(segs):
        outs = [out_refs[o] for o in oidx]

        @pl.when(j == s)
        def _epilogue(kind=kind, outs=outs):
            if kind == "plain":
                for o in outs:
                    o[...] = acc.astype(o.dtype)
            elif kind == "rope":
                cos2, sin2 = cos_ref[...], sin_ref[...]
                for c in range(tn // HEAD_DIM):
                    sl = slice(c * HEAD_DIM, (c + 1) * HEAD_DIM)
                    y = _rope_apply(acc[:, sl], cos2, sin2)
                    for o in outs:
                        o[:, sl] = y.astype(o.dtype)
            elif kind == "kiwi":
                y = _rope_apply(acc[:, :IDX_DIM], cos_ref[...], sin_ref[...])
                outs[0][...] = y
                outs[1][...] = y.astype(BF16)
                outs[2][...] = acc[:, IDX_DIM:] * wi_scale
            else:
                raise ValueError(kind)


def _norm_proj(x, gain, w, tn, segs, outs, rope=None, wi_scale=1.0, tm=512):
    m, k = x.shape
    tm = min(tm, m)
    n_seg = len(segs)
    assert w.shape == (k, n_seg * tn) and m % tm == 0
    in_specs = [pl.BlockSpec((tm, k), lambda i, j: (i, 0)),
                pl.BlockSpec((1, k), lambda i, j: (0, 0)),
                pl.BlockSpec((k, tn), lambda i, j: (0, j))]
    args = [x, gain, w]
    if rope is not None:
        in_specs += [pl.BlockSpec((tm, HEAD_DIM), lambda i, j: (i, 0))] * 2
        args += list(rope)
    out_specs, out_shapes = [], []
    for dt, start, cnt in outs:
        width = tn if segs[start][0] != "kiwi" else IDX_DIM
        out_specs.append(pl.BlockSpec(
            (tm, width), lambda i, j, start=start, cnt=cnt: (i, jnp.clip(j - start, 0, cnt - 1))))
        out_shapes.append(jax.ShapeDtypeStruct((m, width * cnt), dt))
    body = functools.partial(_proj_body, segs=segs, n_out=len(outs), has_rope=rope is not None, wi_scale=wi_scale)
    return pl.pallas_call(
        body, grid=(m // tm, n_seg), in_specs=in_specs, out_specs=out_specs, out_shape=out_shapes,
        scratch_shapes=[pltpu.VMEM((tm, k), BF16)], name="norm_proj_" + segs[-1][0],
        compiler_params=_cparams(("arbitrary", "arbitrary")))(*args)


def _sb_prompt_body(q_ref, k_ref, v_ref, tri_ref, o_ref, *, tq, scale):
    i = pl.program_id(1)
    q = q_ref[...]
    tri = tri_ref[...]

    def block(kb, acc, carry, masked):
        start = pl.multiple_of(kb * tq, tq)
        k = k_ref[pl.ds(start, tq), :]
        v = v_ref[pl.ds(start, tq), :]
        z = lax.dot_general(q, k, (((1,), (1,)), ((), ())), preferred_element_type=F32) * scale
        lg = -_softplus(z)
        if masked:
            row = lax.broadcasted_iota(I32, (tq, tq), 0)
            col = lax.broadcasted_iota(I32, (tq, tq), 1)
            keep = col < row
            lg = jnp.where(keep, lg, 0.0)
        suffix = _split_dot(lg, tri)
        w = jnp.exp(z + lg + suffix + carry)
        if masked:
            w = jnp.where(keep, w, 0.0)
        acc = acc + jnp.dot(w.astype(BF16), v, preferred_element_type=F32)
        carry = carry + jnp.sum(lg, axis=1, keepdims=True)
        return acc, carry

    acc, carry = block(i, jnp.zeros((tq, HEAD_DIM), F32), jnp.zeros((tq, 1), F32), True)

    def more(c):
        return jnp.logical_and(c[0] >= 0, jnp.max(c[2]) > EXP_ZERO_BELOW)

    def step(c):
        acc, carry = block(c[0], c[1], c[2], False)
        return c[0] - 1, acc, carry

    _, acc, carry = lax.while_loop(more, step, (i - 1, acc, carry))
    o_ref[...] = acc.astype(o_ref.dtype)


def _sb_prompt(q, k, v, n_heads, tq=256):
    s = q.shape[0]
    tq = min(tq, s)
    tri = (lax.broadcasted_iota(I32, (tq, tq), 0) > lax.broadcasted_iota(I32, (tq, tq), 1)).astype(BF16)
    body = functools.partial(_sb_prompt_body, tq=tq, scale=HEAD_DIM ** -0.5)
    return pl.pallas_call(
        body, grid=(n_heads, s // tq), name="sb_prompt",
        in_specs=[pl.BlockSpec((tq, HEAD_DIM), lambda h, i: (i, h)),
                  pl.BlockSpec((s, HEAD_DIM), lambda h, i: (0, h)),
                  pl.BlockSpec((s, HEAD_DIM), lambda h, i: (0, h)),
                  pl.BlockSpec((tq, tq), lambda h, i: (0, 0))],
        out_specs=pl.BlockSpec((tq, HEAD_DIM), lambda h, i: (i, h)),
        out_shape=jax.ShapeDtypeStruct((s, n_heads * HEAD_DIM), BF16),
        compiler_params=_cparams(("arbitrary", "arbitrary")))(q, k, v, tri)


def _sort_key(score):
    b = lax.bitcast_convert_type(score, I32)
    return jnp.where(b < 0, (b ^ jnp.int32(0x7FFFFFFF)) + 1, b)


def _key_value(key):
    return lax.bitcast_convert_type(jnp.where(key < 0, (key - 1) ^ jnp.int32(0x7FFFFFFF), key), F32)


def _topk_threshold(count_ge, key_min, key_max, n_keys, topk):
    k = float(topk)

    def undecided(lo, hi, cnt):
        return jnp.logical_and(cnt > k, hi > lo + 1)

    def more(c):
        it, lo, hi, cnt = c
        return jnp.max(undecided(lo, hi, cnt).astype(F32)) > 0.0

    def step(c):
        it, lo, hi, cnt = c
        mid_v = jnp.clip(_sort_key(0.5 * _key_value(lo) + 0.5 * _key_value(hi)), lo + 1, hi - 1)
        mid_i = lo + lax.shift_right_logical(hi - lo, 1)
        mid = jnp.where(it < VALUE_BISECT_STEPS, mid_v, mid_i)
        mid = jnp.where(undecided(lo, hi, cnt), mid, lo)
        c_mid = count_ge(mid)
        ge = c_mid >= k
        return it + 1, jnp.where(ge, mid, lo), jnp.where(ge, hi, mid), jnp.where(ge, c_mid, cnt)

    _, lo, _, _ = lax.while_loop(more, step, (jnp.int32(0), key_min, key_max + 1, n_keys))
    return lo


def _dsa_prompt_body(qi_ref, ki_ref, wi_ref, q_ref, k_ref, v_ref, o_ref, key_ref, bias_ref, wb_ref,
                     *, tq, tk, n_idx_heads, topk, scale):
    i = pl.program_id(0)
    h = pl.program_id(1)
    n_chunks = (i * tq + tq - 1) // tk + 1
    qpos = i * tq + lax.broadcasted_iota(I32, (tq, tk), 0)

    @pl.when(h == 0)
    def _select():
        for hh in range(n_idx_heads):
            wb_ref[hh] = jnp.broadcast_to(wi_ref[:, hh:hh + 1], (tq, LANES))

        def score_chunk(c, _):
            start = pl.multiple_of(c * tk, tk)
            ki = ki_ref[pl.ds(start, tk), :]
            acc = jnp.zeros((tq, tk), F32)
            for hh in range(n_idx_heads):
                s = lax.dot_general(qi_ref[:, hh * IDX_DIM:(hh + 1) * IDX_DIM], ki, (((1,), (1,)), ((), ())),
                                    preferred_element_type=F32)
                wb = wb_ref[hh]
                acc = acc + jnp.maximum(s, 0.0) * jnp.concatenate([wb] * (tk // LANES), axis=1)
            kpos = start + lax.broadcasted_iota(I32, (tq, tk), 1)
            key_ref[c] = jnp.where(kpos <= qpos, _sort_key(acc), INT_MIN)
            return 0

        lax.fori_loop(0, n_chunks, score_chunk, 0)

        rb = min(SELECT_ROWS, tq)

        def row_pass(fn, init, finish):
            outs = []
            for r0 in range(0, tq, rb):
                def chunk(c, part, r0=r0):
                    keys = key_ref[c, r0:r0 + rb, :]
                    for u in range(tk // LANES):
                        part = fn(part, keys[:, u * LANES:(u + 1) * LANES], r0)
                    return part

                outs.append(finish(lax.fori_loop(0, n_chunks, chunk, jnp.full((rb, LANES), init, I32))))
            return jnp.concatenate(outs, axis=0)

        def count_ge(cand):
            cb = {r0: jnp.broadcast_to(cand[r0:r0 + rb], (rb, LANES)) for r0 in range(0, tq, rb)}
            return row_pass(lambda part, keys, r0: part + (keys >= cb[r0]).astype(I32), 0,
                            lambda part: jnp.sum(part.astype(F32), axis=1, keepdims=True))

        key_max = row_pass(lambda part, keys, r0: jnp.maximum(part, keys), KEY_NEG_INF,
                           lambda part: _sort_key(jnp.max(_key_value(part), axis=1, keepdims=True)))
        key_min = row_pass(lambda part, keys, r0: jnp.minimum(part, jnp.where(keys == INT_MIN, KEY_POS_INF, keys)),
                           KEY_POS_INF, lambda part: _sort_key(jnp.min(_key_value(part), axis=1, keepdims=True)))
        n_keys = (i * tq + 1 + lax.broadcasted_iota(I32, (tq, 1), 0)).astype(F32)
        thr = _topk_threshold(count_ge, key_min, key_max, n_keys, topk)

        def bias_chunk(c, _):
            bias_ref[c] = jnp.where(key_ref[c] >= thr, 0.0, NEG_BIG)
            return 0

        lax.fori_loop(0, n_chunks, bias_chunk, 0)

    q = q_ref[...]

    def attn_chunk(c, carry):
        m, l, acc = carry
        start = pl.multiple_of(c * tk, tk)
        k = k_ref[pl.ds(start, tk), :]
        v = v_ref[pl.ds(start, tk), :]
        s = lax.dot_general(q, k, (((1,), (1,)), ((), ())), preferred_element_type=F32) * scale
        s = s + bias_ref[c]
        m_new = jnp.maximum(m, jnp.max(s, axis=1, keepdims=True))
        alpha = jnp.exp(m - m_new)
        p = jnp.exp(s - m_new)
        l = alpha * l + jnp.sum(p, axis=1, keepdims=True)
        acc = alpha * acc + jnp.dot(p.astype(BF16), v, preferred_element_type=F32)
        return m_new, l, acc

    m, l, acc = lax.fori_loop(
        0, n_chunks, attn_chunk,
        (jnp.full((tq, 1), -jnp.inf, F32), jnp.zeros((tq, 1), F32), jnp.zeros((tq, HEAD_DIM), F32)))
    o_ref[...] = (acc / l).astype(o_ref.dtype)


def _dsa_prompt(qi, ki, wi, q, k, v, n_heads, n_idx_heads, topk, tq=256, tk=512):
    s = q.shape[0]
    tq = min(tq, s)
    tk = min(tk, s)
    body = functools.partial(_dsa_prompt_body, tq=tq, tk=tk, n_idx_heads=n_idx_heads, topk=topk,
                             scale=HEAD_DIM ** -0.5)
    return pl.pallas_call(
        body, grid=(s // tq, n_heads), name="dsa_prompt",
        in_specs=[pl.BlockSpec((tq, n_idx_heads * IDX_DIM), lambda i, h: (i, 0)),
                  pl.BlockSpec((s, IDX_DIM), lambda i, h: (0, 0)),
                  pl.BlockSpec((tq, LANES), lambda i, h: (i, 0)),
                  pl.BlockSpec((tq, HEAD_DIM), lambda i, h: (i, h)),
                  pl.BlockSpec((s, HEAD_DIM), lambda i, h: (0, h)),
                  pl.BlockSpec((s, HEAD_DIM), lambda i, h: (0, h))],
        out_specs=pl.BlockSpec((tq, HEAD_DIM), lambda i, h: (i, h)),
        out_shape=jax.ShapeDtypeStruct((s, n_heads * HEAD_DIM), BF16),
        scratch_shapes=[pltpu.VMEM((s // tk, tq, tk), I32), pltpu.VMEM((s // tk, tq, tk), F32),
                        pltpu.VMEM((n_idx_heads, tq, LANES), F32)],
        compiler_params=_cparams(("arbitrary", "arbitrary")))(qi, ki, wi, q, k, v)


def _flat_head_mask(n_heads, width):
    row = lax.broadcasted_iota(I32, (HEAD_ROWS, width), 0)
    col = lax.broadcasted_iota(I32, (HEAD_ROWS, width), 1)
    return row == lax.rem(col, n_heads)


def _pad_head_rows(x):
    return jnp.concatenate([x, jnp.zeros((HEAD_ROWS - x.shape[0], x.shape[1]), x.dtype)], axis=0)


def _flat_scores(qb, page, hmask):
    k2 = page.reshape(page.shape[0] * page.shape[1], page.shape[2]).astype(BF16)
    z = lax.dot_general(qb, k2, (((1,), (1,)), ((), ())), preferred_element_type=F32)
    return jnp.sum(jnp.where(hmask, z, 0.0), axis=0, keepdims=True)


def _flat_weighted_values(w, page, hmask):
    v2 = page.reshape(page.shape[0] * page.shape[1], page.shape[2]).astype(BF16)
    wm = jnp.where(hmask, jnp.broadcast_to(w, hmask.shape), 0.0).astype(BF16)
    return jnp.dot(wm, v2, preferred_element_type=F32)


def _class_reduce(x, n_heads, op):
    y = x[:, :LANES]
    for u in range(1, x.shape[1] // LANES):
        y = op(y, x[:, u * LANES:(u + 1) * LANES])
    if y.shape[0] > 1:
        y = (jnp.max if op is jnp.maximum else jnp.sum)(y, axis=0, keepdims=True)
    shift = n_heads
    while shift < LANES:
        y = op(y, pltpu.roll(y, shift, axis=1))
        shift *= 2
    return y


def _class_to_rows(c):
    row = lax.broadcasted_iota(I32, (HEAD_ROWS, LANES), 0)
    lane = lax.broadcasted_iota(I32, (HEAD_ROWS, LANES), 1)
    return jnp.sum(jnp.where(row == lane, jnp.broadcast_to(c, (HEAD_ROWS, LANES)), 0.0), axis=1, keepdims=True)


def _rows_to_class(r, n_heads):
    row = lax.broadcasted_iota(I32, (HEAD_ROWS, LANES), 0)
    lane = lax.broadcasted_iota(I32, (HEAD_ROWS, LANES), 1)
    return jnp.sum(jnp.where(row == lax.rem(lane, n_heads), jnp.broadcast_to(r, (HEAD_ROWS, LANES)), 0.0),
                   axis=0, keepdims=True)


def _paged_dsa_body(pt_ref, q_ref, bias_ref, bnew_ref, knew_ref, vnew_ref, rep_ref, *refs, pp, steps, scale):
    del pt_ref
    k_refs, v_refs = refs[:pp], refs[pp:2 * pp]
    o_ref, s_ref, acc_ref, lw_ref = refs[2 * pp:]
    p = pl.program_id(1)
    n_heads = q_ref.shape[0]
    flat = PAGE_SIZE * n_heads
    hmask = _flat_head_mask(n_heads, flat)
    qf = _pad_head_rows(q_ref[...])
    qb = qf.astype(BF16)

    @pl.when(p < steps)
    def _scores():
        z = jnp.concatenate([_flat_scores(qb, k_refs[r][...], hmask) for r in range(pp)], axis=0)
        bias = jnp.dot(bias_ref[...].astype(BF16), rep_ref[...], preferred_element_type=F32)
        s_ref[pl.ds(pl.multiple_of(p * pp, pp), pp), :] = z * scale + bias

    @pl.when(p == steps - 1)
    def _softmax():
        s_new = jnp.sum(qf * _pad_head_rows(knew_ref[...]), axis=1, keepdims=True) * scale + bnew_ref[:, 0:1]
        s = s_ref[...]
        m_rows = jnp.maximum(_class_to_rows(_class_reduce(s, n_heads, jnp.maximum)), s_new)
        m_cls = _rows_to_class(m_rows, n_heads)
        w = jnp.exp(s - jnp.concatenate([m_cls] * n_heads, axis=1))
        s_ref[...] = w
        w_new = jnp.exp(s_new - m_rows)
        l_rows = _class_to_rows(_class_reduce(w, n_heads, jnp.add)) + w_new
        lw_ref[:, 0:1] = l_rows
        lw_ref[:, 1:2] = w_new
        acc_ref[...] = jnp.zeros_like(acc_ref)

    @pl.when(p >= steps)
    def _values():
        w8 = s_ref[pl.ds(pl.multiple_of((p - steps) * pp, pp), pp), :]
        acc = acc_ref[...]
        for r in range(pp):
            acc += _flat_weighted_values(w8[r:r + 1], v_refs[r][...], hmask)
        acc_ref[...] = acc

    @pl.when(p == 2 * steps - 1)
    def _finish():
        out = (acc_ref[...] + lw_ref[:, 1:2] * _pad_head_rows(vnew_ref[...])) / lw_ref[:, 0:1]
        o_ref[...] = out[:n_heads]


def _paged_dsa(q, cache_k, cache_v, page_table, bias, bias_new, k_new, v_new, pp=8):
    b, n_heads, hd = q.shape
    n_pages = page_table.shape[1]
    pp = min(pp, n_pages)
    assert n_pages % pp == 0 and pp % 8 == 0 and LANES % n_heads == 0 and n_heads <= HEAD_ROWS
    steps = n_pages // pp
    flat = PAGE_SIZE * n_heads
    rep = (lax.broadcasted_iota(I32, (PAGE_SIZE, flat), 0)
           == lax.broadcasted_iota(I32, (PAGE_SIZE, flat), 1) // n_heads).astype(BF16)
    head_spec = pl.BlockSpec((None, n_heads, hd), lambda bi, p, pt: (bi, 0, 0))
    k_specs = [pl.BlockSpec((None, PAGE_SIZE, n_heads, hd),
                            lambda bi, p, pt, r=r: (pt[bi, jnp.minimum(p, steps - 1) * pp + r], 0, 0, 0))
               for r in range(pp)]
    v_specs = [pl.BlockSpec((None, PAGE_SIZE, n_heads, hd),
                            lambda bi, p, pt, r=r: (pt[bi, jnp.maximum(p - steps, 0) * pp + r], 0, 0, 0))
               for r in range(pp)]
    grid_spec = pltpu.PrefetchScalarGridSpec(
        num_scalar_prefetch=1, grid=(b, 2 * steps),
        in_specs=[head_spec,
                  pl.BlockSpec((None, pp, PAGE_SIZE), lambda bi, p, pt: (bi, jnp.minimum(p, steps - 1), 0)),
                  pl.BlockSpec((None, 1, LANES), lambda bi, p, pt: (bi, 0, 0)), head_spec, head_spec,
                  pl.BlockSpec((PAGE_SIZE, flat), lambda bi, p, pt: (0, 0))]
        + k_specs + v_specs,
        out_specs=head_spec,
        scratch_shapes=[pltpu.VMEM((n_pages, flat), F32), pltpu.VMEM((HEAD_ROWS, hd), F32),
                        pltpu.VMEM((HEAD_ROWS, LANES), F32)])
    body = functools.partial(_paged_dsa_body, pp=pp, steps=steps, scale=HEAD_DIM ** -0.5)
    return pl.pallas_call(
        body, grid_spec=grid_spec, out_shape=jax.ShapeDtypeStruct((b, n_heads, hd), F32), name="paged_dsa",
        compiler_params=_cparams(("arbitrary", "arbitrary")))(
            page_table, q, bias, bias_new, k_new, v_new, rep, *([cache_k] * pp), *([cache_v] * pp))


def _paged_sb_body(pt_ref, q_ref, tri_ref, ck_hbm, cv_hbm, o_ref, kbuf, vbuf, sem, *, n_pages, scale):
    b = pl.program_id(0)
    n_heads, hd = q_ref.shape
    flat = PAGE_SIZE * n_heads
    hmask = _flat_head_mask(n_heads, flat)
    qb = _pad_head_rows(q_ref[...]).astype(BF16)
    row = lax.broadcasted_iota(I32, (HEAD_ROWS, flat), 0)

    def page_copies(page, slot):
        phys = pt_ref[b, page]
        return (pltpu.make_async_copy(ck_hbm.at[phys], kbuf.at[slot], sem.at[0, slot]),
                pltpu.make_async_copy(cv_hbm.at[phys], vbuf.at[slot], sem.at[1, slot]))

    def slot_of(page):
        return lax.rem(n_pages - 1 - page, 2)

    for cp in page_copies(n_pages - 1, 0):
        cp.start()

    def more(c):
        return jnp.logical_and(c[0] >= 0, jnp.max(c[2]) > EXP_ZERO_BELOW)

    def step(c):
        page, acc, carry = c
        slot = slot_of(page)
        for cp in page_copies(page, slot):
            cp.wait()

        @pl.when(page > 0)
        def _prefetch():
            for cp in page_copies(page - 1, 1 - slot):
                cp.start()

        z = _flat_scores(qb, kbuf[slot], hmask) * scale
        lg = -_softplus(z)
        hi = lg.astype(BF16).astype(F32)
        parts = jnp.where(row == 0, jnp.broadcast_to(hi, row.shape),
                          jnp.where(row == 1, jnp.broadcast_to(lg - hi, row.shape), 0.0)).astype(BF16)
        suffix = jnp.sum(jnp.dot(parts, tri_ref[...], preferred_element_type=F32), axis=0, keepdims=True)
        w = jnp.exp(z + lg + suffix + jnp.concatenate([carry] * n_heads, axis=1))
        acc = acc + _flat_weighted_values(w, vbuf[slot], hmask)
        return page - 1, acc, carry + _class_reduce(lg, n_heads, jnp.add)

    page, acc, _ = lax.while_loop(
        more, step, (jnp.int32(n_pages - 1), jnp.zeros((HEAD_ROWS, hd), F32), jnp.zeros((1, LANES), F32)))

    @pl.when(page >= 0)
    def _drain():
        for cp in page_copies(page, slot_of(page)):
            cp.wait()

    o_ref[...] = acc[:n_heads]


def _paged_sb(q, cache_k, cache_v, page_table):
    b, n_heads, hd = q.shape
    n_pages = page_table.shape[1]
    flat = PAGE_SIZE * n_heads
    assert LANES % n_heads == 0 and n_heads <= HEAD_ROWS
    ia = lax.broadcasted_iota(I32, (flat, flat), 0)
    ic = lax.broadcasted_iota(I32, (flat, flat), 1)
    tri = jnp.logical_and(ia % n_heads == ic % n_heads, ia // n_heads > ic // n_heads).astype(BF16)
    head_spec = pl.BlockSpec((None, n_heads, hd), lambda bi, pt: (bi, 0, 0))
    grid_spec = pltpu.PrefetchScalarGridSpec(
        num_scalar_prefetch=1, grid=(b,),
        in_specs=[head_spec, pl.BlockSpec((flat, flat), lambda bi, pt: (0, 0)),
                  pl.BlockSpec(memory_space=pl.ANY), pl.BlockSpec(memory_space=pl.ANY)],
        out_specs=head_spec,
        scratch_shapes=[pltpu.VMEM((2, PAGE_SIZE, n_heads, hd), F32), pltpu.VMEM((2, PAGE_SIZE, n_heads, hd), F32),
                        pltpu.SemaphoreType.DMA((2, 2))])
    body = functools.partial(_paged_sb_body, n_pages=n_pages, scale=HEAD_DIM ** -0.5)
    return pl.pallas_call(
        body, grid_spec=grid_spec, out_shape=jax.ShapeDtypeStruct((b, n_heads, hd), F32), name="paged_sb",
        compiler_params=_cparams(("arbitrary",)))(page_table, q, tri, cache_k, cache_v)


def _idx_scores_body(pt_ref, qi_ref, wcol_ref, knew_ref, *refs, pp):
    del pt_ref
    page_refs = refs[:pp]
    o_ref, onew_ref = refs[pp:pp + 2]
    qi = qi_ref[...]
    qb = qi.astype(BF16)
    wcol = wcol_ref[...]
    for r in range(pp):
        kp = page_refs[r][...].astype(BF16)
        s = lax.dot_general(qb, kp, (((1,), (1,)), ((), ())), preferred_element_type=F32)
        o_ref[:, r * PAGE_SIZE:(r + 1) * PAGE_SIZE] = jnp.sum(jnp.maximum(s, 0.0) * wcol, axis=0, keepdims=True)

    @pl.when(pl.program_id(1) == 0)
    def _new_key():
        s = jnp.sum(qi * knew_ref[...], axis=1, keepdims=True)
        sc = jnp.sum(jnp.maximum(s, 0.0) * wcol[:, 0:1], axis=0, keepdims=True)
        lane = lax.broadcasted_iota(I32, (1, LANES), 1)
        onew_ref[...] = jnp.where(lane == 0, sc, -jnp.inf)


def _idx_scores_sample(qi, wcol, ki_new, cache_ki, page_table, pp=8):
    b, hi, e = qi.shape
    n_pages = page_table.shape[1]
    pp = min(pp, n_pages)
    steps = n_pages // pp

    def page_map(r):
        return lambda bi, p, pt: (pt[bi, p * pp + r], 0, 0)

    grid_spec = pltpu.PrefetchScalarGridSpec(
        num_scalar_prefetch=1, grid=(b, steps),
        in_specs=[pl.BlockSpec((None, hi, e), lambda bi, p, pt: (bi, 0, 0)),
                  pl.BlockSpec((None, hi, LANES), lambda bi, p, pt: (bi, 0, 0)),
                  pl.BlockSpec((None, 1, e), lambda bi, p, pt: (bi, 0, 0))]
        + [pl.BlockSpec((None, PAGE_SIZE, e), page_map(r)) for r in range(pp)],
        out_specs=[pl.BlockSpec((None, 1, pp * PAGE_SIZE), lambda bi, p, pt: (bi, 0, p)),
                   pl.BlockSpec((None, 1, LANES), lambda bi, p, pt: (bi, 0, 0))])
    return pl.pallas_call(
        functools.partial(_idx_scores_body, pp=pp), grid_spec=grid_spec, name="idx_scores_sample",
        out_shape=[jax.ShapeDtypeStruct((b, 1, n_pages * PAGE_SIZE), F32), jax.ShapeDtypeStruct((b, 1, LANES), F32)],
        compiler_params=_cparams(("arbitrary", "arbitrary")))(page_table, qi, wcol, ki_new, *([cache_ki] * pp))


def _topk_bias_body(s_ref, o_ref, *, topk):
    keys = _sort_key(s_ref[...])
    keys = jnp.where(s_ref[...] == -jnp.inf, INT_MIN, keys)

    def count_ge(cand):
        return jnp.sum((keys >= cand).astype(F32), axis=1, keepdims=True)

    real = s_ref[...] != -jnp.inf
    key_max = _sort_key(jnp.max(s_ref[...], axis=1, keepdims=True))
    key_min = _sort_key(jnp.min(jnp.where(real, s_ref[...], jnp.inf), axis=1, keepdims=True))
    n_keys = jnp.sum(real.astype(F32), axis=1, keepdims=True)
    thr = _topk_threshold(count_ge, key_min, key_max, n_keys, topk)
    o_ref[...] = jnp.where(keys >= thr, 0.0, NEG_BIG)


def _topk_bias(scores, topk):
    return pl.pallas_call(
        functools.partial(_topk_bias_body, topk=topk), name="topk_bias",
        out_shape=jax.ShapeDtypeStruct(scores.shape, F32))(scores)


def _merge_gate_body(osb_ref, od_ref, wsb_ref, wd_ref, gsb_ref, gd_ref, o_ref):
    y_sb = jnp.dot(osb_ref[...], wsb_ref[...], preferred_element_type=F32)
    y_d = jnp.dot(od_ref[...], wd_ref[...], preferred_element_type=F32)
    s_sb = 0.5 * jnp.tanh(0.5 * gsb_ref[...]) + 0.5
    s_d = 0.5 * jnp.tanh(0.5 * gd_ref[...]) + 0.5
    o_ref[...] = (s_sb * y_sb + s_d * y_d).astype(o_ref.dtype)


def _merge_gate(o_sb, o_d, w_sb, w_d, gates, tm=512, tn=1024):
    m, ksb = o_sb.shape
    kd = o_d.shape[1]
    d = w_sb.shape[1]
    tm = min(tm, m)
    tn = min(tn, d)
    nj = d // tn
    return pl.pallas_call(
        _merge_gate_body, grid=(m // tm, nj), name="merge_gate",
        in_specs=[pl.BlockSpec((tm, ksb), lambda i, j: (i, 0)),
                  pl.BlockSpec((tm, kd), lambda i, j: (i, 0)),
                  pl.BlockSpec((ksb, tn), lambda i, j: (0, j)),
                  pl.BlockSpec((kd, tn), lambda i, j: (0, j)),
                  pl.BlockSpec((tm, tn), lambda i, j: (i, j)),
                  pl.BlockSpec((tm, tn), lambda i, j: (i, nj + j))],
        out_specs=pl.BlockSpec((tm, tn), lambda i, j: (i, j)),
        out_shape=jax.ShapeDtypeStruct((m, d), BF16),
        compiler_params=_cparams(("arbitrary", "arbitrary")))(o_sb, o_d, w_sb, w_d, gates, gates)


def _matmul_res_body(a_ref, w_ref, r_ref, g_ref, *refs, emit_sum, emit_norm):
    outs = refs[:-1]
    acc_ref = refs[-1]
    kk = pl.program_id(1)

    @pl.when(kk == 0)
    def _init():
        acc_ref[...] = jnp.zeros_like(acc_ref)

    acc_ref[...] += jnp.dot(a_ref[...], w_ref[...], preferred_element_type=F32)

    @pl.when(kk == pl.num_programs(1) - 1)
    def _finish():
        y = r_ref[...] + acc_ref[...]
        o = 0
        if emit_sum:
            outs[o][...] = y
            o += 1
        if emit_norm:
            ms = jnp.mean(y * y, axis=-1, keepdims=True)
            outs[o][...] = ((y * lax.rsqrt(ms + NORM_EPS)) * g_ref[...]).astype(outs[o].dtype)


def _matmul_res(a, w, res, gain, *, emit_sum, norm_dtype, tm=512, tk=None):
    m, k = a.shape
    n = w.shape[1]
    tm = min(tm, m)
    tk = k if tk is None else tk
    assert k % tk == 0
    out_shapes, out_specs = [], []
    if emit_sum:
        out_shapes.append(jax.ShapeDtypeStruct((m, n), F32))
    if norm_dtype is not None:
        out_shapes.append(jax.ShapeDtypeStruct((m, n), norm_dtype))
    out_specs = [pl.BlockSpec((tm, n), lambda i, kk: (i, 0)) for _ in out_shapes]
    body = functools.partial(_matmul_res_body, emit_sum=emit_sum, emit_norm=norm_dtype is not None)
    return pl.pallas_call(
        body, grid=(m // tm, k // tk), name="matmul_res",
        in_specs=[pl.BlockSpec((tm, tk), lambda i, kk: (i, kk)),
                  pl.BlockSpec((tk, n), lambda i, kk: (kk, 0)),
                  pl.BlockSpec((tm, n), lambda i, kk: (i, 0)),
                  pl.BlockSpec((1, n), lambda i, kk: (0, 0))],
        out_specs=out_specs, out_shape=out_shapes,
        scratch_shapes=[pltpu.VMEM((tm, n), F32)],
        compiler_params=_cparams(("arbitrary", "arbitrary")))(a, w, res, gain)


def _silu_gate(cg, cv):
    return (cg * jax.nn.sigmoid(cg)) * cv


def _ffn_up_prompt_body(h_ref, wg_ref, wv_ref, cwg_ref, cwv_ref, cbg_ref, cbv_ref, pg_ref, pv_ref,
                        a_ref, og_ref, ov_ref, carry_g, carry_v, *, tm):
    mi = pl.program_id(1)
    h = h_ref[...]
    row = lax.broadcasted_iota(I32, (tm, 1), 0)

    @pl.when(mi == 0)
    def _init():
        carry_g[0:2, :] = pg_ref[...]
        carry_v[0:2, :] = pv_ref[...]

    def conv(w_ref, cw_ref, cb_ref, carry):
        u = jnp.dot(h, w_ref[...], preferred_element_type=F32)
        c_prev, c_last = carry[0:1, :], carry[1:2, :]
        u1 = jnp.where(row == 0, c_last, pltpu.roll(u, 1, axis=0))
        u2 = jnp.where(row == 0, c_prev, jnp.where(row == 1, c_last, pltpu.roll(u, 2, axis=0)))
        carry[0:2, :] = u[tm - 2:tm, :]
        return cb_ref[...] + u2 * cw_ref[0:1, :] + u1 * cw_ref[1:2, :] + u * cw_ref[2:3, :]

    cg = conv(wg_ref, cwg_ref, cbg_ref, carry_g)
    cv = conv(wv_ref, cwv_ref, cbv_ref, carry_v)
    a_ref[...] = _silu_gate(cg, cv).astype(a_ref.dtype)

    @pl.when(mi == pl.num_programs(1) - 1)
    def _state():
        og_ref[...] = carry_g[0:2, :]
        ov_ref[...] = carry_v[0:2, :]


def _ffn_up_prompt(h, w_up, conv_w, conv_b, conv_prev, tm=1024, tf=512):
    m, d = h.shape
    f = w_up.shape[1] // 2
    tm = min(tm, m)
    tf = min(tf, f)
    nf = f // tf
    assert f % tf == 0 and m % tm == 0
    lo = lambda fi, mi: (0, fi)
    hi = lambda fi, mi: (0, nf + fi)
    body = functools.partial(_ffn_up_prompt_body, tm=tm)
    return pl.pallas_call(
        body, grid=(nf, m // tm), name="ffn_up_prompt",
        in_specs=[pl.BlockSpec((tm, d), lambda fi, mi: (mi, 0)),
                  pl.BlockSpec((d, tf), lo), pl.BlockSpec((d, tf), hi),
                  pl.BlockSpec((3, tf), lo), pl.BlockSpec((3, tf), hi),
                  pl.BlockSpec((1, tf), lo), pl.BlockSpec((1, tf), hi),
                  pl.BlockSpec((2, tf), lo), pl.BlockSpec((2, tf), hi)],
        out_specs=[pl.BlockSpec((tm, tf), lambda fi, mi: (mi, fi)),
                   pl.BlockSpec((2, tf), lo), pl.BlockSpec((2, tf), lo)],
        out_shape=[jax.ShapeDtypeStruct((m, f), BF16), jax.ShapeDtypeStruct((2, f), F32),
                   jax.ShapeDtypeStruct((2, f), F32)],
        scratch_shapes=[pltpu.VMEM((8, tf), F32), pltpu.VMEM((8, tf), F32)],
        compiler_params=_cparams(("arbitrary", "arbitrary")))(
            h, w_up, w_up, conv_w, conv_w, conv_b, conv_b, conv_prev, conv_prev)


def _ffn_up_sample_body(h_ref, wg_ref, wv_ref, cwg_ref, cwv_ref, cbg_ref, cbv_ref,
                        s0g_ref, s0v_ref, s1g_ref, s1v_ref, a_ref, ug_ref, uv_ref):
    h = h_ref[...]

    def conv(w_ref, cw_ref, cb_ref, s0_ref, s1_ref, u_ref):
        u = jnp.dot(h, w_ref[...], preferred_element_type=F32)
        u_ref[...] = u
        return cb_ref[...] + s0_ref[...] * cw_ref[0:1, :] + s1_ref[...] * cw_ref[1:2, :] + u * cw_ref[2:3, :]

    cg = conv(wg_ref, cwg_ref, cbg_ref, s0g_ref, s1g_ref, ug_ref)
    cv = conv(wv_ref, cwv_ref, cbv_ref, s0v_ref, s1v_ref, uv_ref)
    a_ref[...] = _silu_gate(cg, cv).astype(a_ref.dtype)


def _ffn_up_sample(h, w_up, conv_w, conv_b, state, tf=512):
    b, d = h.shape
    f = w_up.shape[1] // 2
    tf = min(tf, f)
    nf = f // tf
    col = lambda off: (lambda fi: (0, off * nf + fi))
    wspec = lambda off: pl.BlockSpec((d, tf), col(off))
    sspec = lambda off: pl.BlockSpec((b, tf), col(off))
    return pl.pallas_call(
        _ffn_up_sample_body, grid=(nf,), name="ffn_up_sample",
        in_specs=[pl.BlockSpec((b, d), lambda fi: (0, 0)), wspec(0), wspec(1),
                  pl.BlockSpec((3, tf), col(0)), pl.BlockSpec((3, tf), col(1)),
                  pl.BlockSpec((1, tf), col(0)), pl.BlockSpec((1, tf), col(1)),
                  sspec(0), sspec(1), sspec(2), sspec(3)],
        out_specs=[sspec(0), sspec(0), sspec(0)],
        out_shape=[jax.ShapeDtypeStruct((b, f), BF16), jax.ShapeDtypeStruct((b, f), F32),
                   jax.ShapeDtypeStruct((b, f), F32)],
        compiler_params=_cparams(("arbitrary",)))(
            h, w_up, w_up, conv_w, conv_w, conv_b, conv_b, state, state, state, state)


def _rope_tables(pos, dim):
    inv = ROPE_THETA ** (-jnp.arange(0, dim, 2, dtype=F32) / dim)
    ang = pos.astype(F32)[:, None] * inv[None, :]
    cos, sin = jnp.cos(ang), jnp.sin(ang)
    return jnp.concatenate([cos, cos], axis=1), jnp.concatenate([-sin, sin], axis=1)


def _project_group(x, gain, wts, rope, n_idx_heads, q_dtype):
    w_sb, w_dsa, w_qi, w_kiwi, w_gate = wts
    tn = w_sb.shape[1] // 3
    q_sb, k_sb, k_sb_b, v_sb, v_sb_b = _norm_proj(
        x, gain, w_sb, tn,
        segs=[("plain", [0]), ("plain", [1, 2]), ("plain", [3, 4])],
        outs=[(q_dtype, 0, 1), (F32, 1, 1), (BF16, 1, 1), (F32, 2, 1), (BF16, 2, 1)])
    q_d, k_d, k_d_b, v_d, v_d_b = _norm_proj(
        x, gain, w_dsa, tn,
        segs=[("rope", [0]), ("rope", [1, 2]), ("plain", [3, 4])],
        outs=[(q_dtype, 0, 1), (F32, 1, 1), (BF16, 1, 1), (F32, 2, 1), (BF16, 2, 1)], rope=rope)
    n_qi = w_qi.shape[1] // tn
    (q_i,) = _norm_proj(x, gain, w_qi, tn, segs=[("rope", [0])] * n_qi, outs=[(q_dtype, 0, n_qi)], rope=rope)
    k_i, k_i_b, w_i = _norm_proj(
        x, gain, w_kiwi, 2 * IDX_DIM, segs=[("kiwi", [0, 1, 2])],
        outs=[(F32, 0, 1), (BF16, 0, 1), (F32, 0, 1)], rope=rope,
        wi_scale=n_idx_heads ** -0.5 * IDX_DIM ** -0.5)
    n_g = w_gate.shape[1] // tn
    (gates,) = _norm_proj(x, gain, w_gate, tn, segs=[("plain", [0])] * n_g, outs=[(F32, 0, n_g)])
    return dict(q_sb=q_sb, k_sb=k_sb, k_sb_b=k_sb_b, v_sb=v_sb, v_sb_b=v_sb_b, q_d=q_d, k_d=k_d, k_d_b=k_d_b,
                v_d=v_d, v_d_b=v_d_b, q_i=q_i, k_i=k_i, k_i_b=k_i_b, w_i=w_i, gates=gates)


def _post_attention(x, o_sb, o_d, gates, w_proj_sb, w_proj_dsa, w_out, norm_ffn):
    merged = _merge_gate(o_sb, o_d, w_proj_sb, w_proj_dsa, gates)
    return _matmul_res(merged, w_out, x, norm_ffn, emit_sum=True, norm_dtype=BF16)


def kernel(x_prompt, x_sample, cache_k_sb, cache_v_sb, cache_k_dsa, cache_v_dsa, cache_k_idx, state_conv,
           page_table, norm_mix, w_in, w_proj_sb, w_proj_dsa, w_out, norm_ffn, w_up, conv_w, conv_b, w_down,
           norm_final):
    b_p, s_p, d_model = x_prompt.shape
    b_s, t_s, _ = x_sample.shape
    depth = w_in.shape[0]
    assert b_p == 1 and t_s == 1 and depth == 1
    n_pool, page, sb_heads, hd = cache_k_sb.shape[1:]
    dsa_heads = cache_k_dsa.shape[3]
    assert hd == HEAD_DIM and page == PAGE_SIZE and cache_k_idx.shape[-1] == IDX_DIM
    sb_w, dsa_w = sb_heads * HEAD_DIM, dsa_heads * HEAD_DIM
    d_ff = w_down.shape[1]
    n_idx_heads = w_in.shape[2] - 3 * sb_w - 3 * dsa_w - IDX_DIM - 2 * d_model
    n_idx_heads = n_idx_heads // (IDX_DIM + 1)
    past = page_table.shape[1] * PAGE_SIZE
    l = 0

    wi = w_in[l]
    c0 = 3 * sb_w
    c1 = c0 + 3 * dsa_w
    c2 = c1 + n_idx_heads * IDX_DIM
    c3 = c2 + IDX_DIM
    c4 = c3 + n_idx_heads
    w_kiwi = jnp.concatenate(
        [wi[:, c2:c4], jnp.zeros((d_model, 2 * IDX_DIM - (c4 - c2)), wi.dtype)], axis=1).astype(BF16)
    wts = (wi[:, :c0].astype(BF16), wi[:, c0:c1].astype(BF16), wi[:, c1:c2].astype(BF16), w_kiwi,
           wi[:, c4:].astype(BF16))
    w_psb, w_pd, w_o = w_proj_sb[l].astype(BF16), w_proj_dsa[l].astype(BF16), w_out[l].astype(BF16)
    w_u, w_dn = w_up[l].astype(BF16), w_down[l].astype(BF16)
    g_mix, g_ffn, g_fin = norm_mix[l][None, :], norm_ffn[l][None, :], norm_final[None, :]
    cw, cb = conv_w[l], conv_b[l][None, :]

    xp = x_prompt[0]
    pp = _project_group(xp, g_mix, wts, _rope_tables(jnp.arange(s_p), HEAD_DIM), n_idx_heads, BF16)
    o_sb_p = _sb_prompt(pp["q_sb"], pp["k_sb_b"], pp["v_sb_b"], sb_heads)
    o_d_p = _dsa_prompt(pp["q_i"], pp["k_i_b"], pp["w_i"], pp["q_d"], pp["k_d_b"], pp["v_d_b"],
                        dsa_heads, n_idx_heads, min(TOPK_MAX, s_p // 4))
    x1_p, h2_p = _post_attention(xp, o_sb_p, o_d_p, pp["gates"], w_psb, w_pd, w_o, g_ffn)
    a_p, cg_p, cv_p = _ffn_up_prompt(h2_p, w_u, cw, cb, jnp.zeros((2, 2 * d_ff), F32))
    (y_p,) = _matmul_res(a_p, w_dn, x1_p, g_fin, emit_sum=False, norm_dtype=F32, tk=d_ff // 4)

    xs = x_sample[:, 0]
    pos_s = jnp.full((b_s,), past, I32)
    ps = _project_group(xs, g_mix, wts, _rope_tables(pos_s, HEAD_DIM), n_idx_heads, F32)
    r3 = lambda z: z[:, None, :]
    ck_sb = cache_k_sb.reshape(depth * n_pool, PAGE_SIZE, sb_heads, HEAD_DIM)
    cv_sb = cache_v_sb.reshape(depth * n_pool, PAGE_SIZE, sb_heads, HEAD_DIM)
    ck_d = cache_k_dsa.reshape(depth * n_pool, PAGE_SIZE, dsa_heads, HEAD_DIM)
    cv_d = cache_v_dsa.reshape(depth * n_pool, PAGE_SIZE, dsa_heads, HEAD_DIM)
    ck_i = cache_k_idx.reshape(depth * n_pool, PAGE_SIZE, IDX_DIM)
    page_table = page_table + l * n_pool
    heads3 = lambda z, n: z.reshape(b_s, n, HEAD_DIM)
    o_sb_s = _paged_sb(heads3(ps["q_sb"], sb_heads), ck_sb, cv_sb, page_table).reshape(b_s, sb_w)
    wcol = jnp.broadcast_to(ps["w_i"][:, :n_idx_heads, None], (b_s, n_idx_heads, LANES))
    sc_past, sc_new = _idx_scores_sample(ps["q_i"].reshape(b_s, n_idx_heads, IDX_DIM), wcol, r3(ps["k_i"]),
                                         ck_i, page_table)
    scores = jnp.concatenate([sc_past[:, 0], sc_new[:, 0]], axis=1)
    bias = _topk_bias(scores, min(TOPK_MAX, (past + t_s) // 4))
    o_d_s = _paged_dsa(heads3(ps["q_d"], dsa_heads), ck_d, cv_d, page_table,
                       bias[:, :past].reshape(b_s, past // PAGE_SIZE, PAGE_SIZE), r3(bias[:, past:]),
                       heads3(ps["k_d"], dsa_heads), heads3(ps["v_d"], dsa_heads)).reshape(b_s, dsa_w)
    x1_s, h2_s = _post_attention(xs, o_sb_s.astype(BF16), o_d_s.astype(BF16), ps["gates"],
                                 w_psb, w_pd, w_o, g_ffn)
    st = state_conv[l].reshape(b_s, 2 * 2 * d_ff)
    a_s, ug_s, uv_s = _ffn_up_sample(h2_s, w_u, cw, cb, st)
    (y_s,) = _matmul_res(a_s, w_dn, x1_s, g_fin, emit_sum=False, norm_dtype=F32, tk=d_ff // 4)

    hd4 = lambda z, n: z.reshape(1, b_p, s_p, n, HEAD_DIM)
    hs4 = lambda z, n: z.reshape(1, b_s, t_s, n, HEAD_DIM)
    conv_p = jnp.concatenate([cg_p, cv_p], axis=1)[None, None]
    conv_s = jnp.stack([state_conv[l][:, 1], jnp.concatenate([ug_s, uv_s], axis=1)], axis=1)[None]
    return (y_p[None], y_s[:, None, :],
            hd4(pp["k_sb"], sb_heads), hd4(pp["v_sb"], sb_heads), hd4(pp["k_d"], dsa_heads),
            hd4(pp["v_d"], dsa_heads), pp["k_i"].reshape(1, b_p, s_p, IDX_DIM), conv_p,
            hs4(ps["k_sb"], sb_heads), hs4(ps["v_sb"], sb_heads), hs4(ps["k_d"], dsa_heads),
            hs4(ps["v_d"], dsa_heads), ps["k_i"].reshape(1, b_s, t_s, IDX_DIM), conv_s)
```
